```python
import math
import jax
import jax.numpy as jnp
from jax import lax
import numpy as np

D_MODEL = 1024
BATCH = 2
SEQ = 8192
DEPTH = 2

GRID_W = 64
CTX_LEN = 256
D_MIX = D_MODEL
GROUP_W = D_MIX // 4
A_HEADS = 4
A_DV = GROUP_W // A_HEADS
A_DK = A_DV // 2
B_HEADS = 4
B_DH = GROUP_W // B_HEADS
CONV_W = 3
C_HEADS = 4
C_DV = GROUP_W // C_HEADS
C_DK = C_DV // 2
GLA_RANK = 16
GLA_TAU = 16.0
D_HEADS = 4
D_KV = 2
D_GROUP = D_HEADS // D_KV
D_HD = GROUP_W // D_HEADS
WINDOW = 128
BLOCK = 128
CHUNK = 64
ROPE_THETA = 10000.0
EPS = 1e-6
F32 = jnp.float32

SPLIT_SIZES = (
    A_HEADS * 2 * A_DK, A_HEADS * 2 * A_DK, A_HEADS * A_DV,
    B_HEADS * B_DH, B_HEADS * B_DH, B_HEADS * B_DH, 4 * B_HEADS, B_HEADS * B_DH,
    C_HEADS * C_DK, C_HEADS * C_DK, C_HEADS * C_DV, 2 * GLA_RANK,
    D_HEADS * D_HD, D_KV * D_HD, D_KV * D_HD,
    D_MIX,
)
N_IN = sum(SPLIT_SIZES)

kernel_name = 'hybrid_parallel_heads_diffusion_block'


def rmsnorm(x, g):
    xf = x.astype(F32)
    y = xf * lax.rsqrt(jnp.mean(xf * xf, axis=-1, keepdims=True) + EPS)
    return (y * g.astype(F32)).astype(x.dtype)


def split_cols(p):
    outs, start = [], 0
    for size in SPLIT_SIZES:
        outs.append(p[..., start:start + size])
        start += size
    return outs


def _rotate(xh, ang):
    x1, x2 = jnp.split(xh, 2, axis=-1)
    cos, sin = jnp.cos(ang), jnp.sin(ang)
    return jnp.concatenate([x1 * cos - x2 * sin, x1 * sin + x2 * cos], axis=-1)


def axial_rope(x, rows, cols):
    hd = x.shape[-1]
    nq = hd // 4
    inv = ROPE_THETA ** (-jnp.arange(nq, dtype=F32) / nq)
    shp = (x.shape[1],) + (1,) * (x.ndim - 3) + (nq,)
    ang_r = (rows.astype(F32)[:, None] * inv).reshape(shp)
    ang_c = (cols.astype(F32)[:, None] * inv).reshape(shp)
    xf = x.astype(F32)
    out = jnp.concatenate([_rotate(xf[..., :hd // 2], ang_r), _rotate(xf[..., hd // 2:], ang_c)], axis=-1)
    return out.astype(x.dtype)


def short_conv(x, w, b):
    ch = x.shape[-1]
    y = lax.conv_general_dilated(x, w.reshape(CONV_W, 1, ch).astype(x.dtype), (1,),
                                 [((CONV_W - 1) // 2, CONV_W // 2)],
                                 dimension_numbers=('NWC', 'WIO', 'NWC'), feature_group_count=ch)
    return y + b.astype(x.dtype)


def _to_chunks(a):
    bsz, nh, t = a.shape[:3]
    a = a.reshape((bsz, nh, t // CHUNK, CHUNK) + a.shape[3:])
    return jnp.moveaxis(a, 2, 0)


def chunked_scan(step, state, seqs):
    t = seqs[0].shape[2]
    state, hs = lax.scan(step, state, tuple(_to_chunks(a) for a in seqs))
    hs = jnp.moveaxis(hs, 0, 2)
    return hs.reshape(hs.shape[:2] + (t,) + hs.shape[4:]), state


def bidir(step, init, ctx_f, lat_f, ctx_b, lat_b):
    flip = lambda seqs: tuple(jnp.flip(a, 2) for a in seqs)
    hc_f, st_f = chunked_scan(step, init, ctx_f)
    hl_f, _ = chunked_scan(step, st_f, lat_f)
    hc_b, st_b = chunked_scan(step, init, flip(ctx_b))
    hl_b, _ = chunked_scan(step, st_b, flip(lat_b))
    return hl_f + jnp.flip(hl_b, 2), hc_f + jnp.flip(hc_b, 2)


def _mlstm_step(state, xs):
    cmat, nvec, m = state
    q, k, v, li, lf = xs
    ln = q.shape[2]
    tril = jnp.tril(jnp.ones((ln, ln), dtype=bool))
    b = jnp.cumsum(lf, axis=-1)
    d = jnp.where(tril, b[..., :, None] - b[..., None, :] + li[..., None, :], -jnp.inf)
    m_inter = b + m[..., None]
    m_t = jnp.maximum(m_inter, jnp.max(d, axis=-1))
    w = jnp.exp(d - m_t[..., None])
    g_inter = jnp.exp(m_inter - m_t)
    s = jnp.einsum('bhtd,bhsd->bhts', q, k) * w
    num = jnp.einsum('bhts,bhsv->bhtv', s, v) + g_inter[..., None] * jnp.einsum('bhvd,bhtd->bhtv', cmat, q)
    nq = jnp.sum(s, axis=-1) + g_inter * jnp.einsum('bhd,bhtd->bht', nvec, q)
    h = num / jnp.maximum(jnp.abs(nq), jnp.exp(-m_t))[..., None]
    b_end = b[..., -1]
    g = b_end[..., None] - b + li
    m_new = jnp.maximum(b_end + m, jnp.max(g, axis=-1))
    w_end = jnp.exp(g - m_new[..., None])
    decay = jnp.exp(b_end + m - m_new)
    cmat = decay[..., None, None] * cmat + jnp.einsum('bhs,bhsv,bhsd->bhvd', w_end, v, k)
    nvec = decay[..., None] * nvec + jnp.einsum('bhs,bhsd->bhd', w_end, k)
    return (cmat, nvec, m_new), h


def _gla_step(smat, xs):
    q, k, v, la = xs
    ln = q.shape[2]
    tril = jnp.tril(jnp.ones((ln, ln), dtype=bool))
    bc = jnp.cumsum(la, axis=2)
    inter = jnp.einsum('bhtd,bhdv->bhtv', q * jnp.exp(bc), smat)
    rel = jnp.where(tril[:, :, None], bc[:, :, :, None, :] - bc[:, :, None, :, :], -jnp.inf)
    a = jnp.einsum('bhtd,bhtsd,bhsd->bhts', q, jnp.exp(rel), k)
    o = inter + jnp.einsum('bhts,bhsv->bhtv', a, v)
    b_end = bc[:, :, -1]
    smat = jnp.exp(b_end)[..., None] * smat + jnp.einsum('bhsd,bhsv->bhdv', k * jnp.exp(b_end[:, :, None] - bc), v)
    return smat, o


def diff_attention(pl, pc, qn_g, kn_g, lam_p, subln_g, lam_init, rows, cols, with_ctx):
    (ql, kl, vl), (qc, kc, vc) = pl, pc
    bsz, t = ql.shape[:2]
    lc = qc.shape[1]

    def heads(q, k, v, n):
        q = rmsnorm(q.reshape(bsz, n, A_HEADS, 2, A_DK), qn_g)
        k = rmsnorm(k.reshape(bsz, n, A_HEADS, 2, A_DK), kn_g)
        return q, k, v.reshape(bsz, n, A_HEADS, A_DV)

    ql, kl, vl = heads(ql, kl, vl, t)
    ql = axial_rope(ql, rows, cols)
    kl = axial_rope(kl, rows, cols)
    qc, kc, vc = heads(qc, kc, vc, lc)
    lp = lam_p.astype(F32)
    lam = jnp.exp(jnp.sum(lp[0] * lp[1])) - jnp.exp(jnp.sum(lp[2] * lp[3])) + lam_init
    scale = A_DK ** -0.5

    def attend(q, k, v):
        s = jnp.einsum('bqhcd,bkhcd->bhcqk', q, k).astype(F32) * scale
        p = jax.nn.softmax(s, axis=-1)
        a = p[:, :, 0] - lam * p[:, :, 1]
        return jnp.einsum('bhqk,bkhd->bqhd', a, v.astype(F32))

    k_all = jnp.concatenate([kc, kl], axis=1)
    v_all = jnp.concatenate([vc, vl], axis=1)
    nb = t // BLOCK
    qb = jnp.moveaxis(ql.reshape(bsz, nb, BLOCK, A_HEADS, 2, A_DK), 1, 0)
    ol = lax.map(lambda qblk: attend(qblk, k_all, v_all), qb)
    ol = jnp.moveaxis(ol, 0, 1).reshape(bsz, t, A_HEADS, A_DV)

    def finish(o):
        return (rmsnorm(o, subln_g) * (1.0 - lam_init)).reshape(o.shape[0], o.shape[1], A_HEADS * A_DV)

    out_c = finish(attend(qc, kc, vc)) if with_ctx else None
    return finish(ol), out_c


def mlstm_mixer(pl, pc, conv_w, conv_b, gate_b, outn_g, with_ctx):
    def prep(q, k, v, gt, o):
        bsz, t = q.shape[:2]
        qk = jax.nn.silu(short_conv(jnp.concatenate([q, k], axis=-1), conv_w, conv_b))
        q, k = jnp.split(qk, 2, axis=-1)
        th = lambda a: a.reshape(bsz, t, B_HEADS, B_DH).transpose(0, 2, 1, 3).astype(F32)
        pre = gt.reshape(bsz, t, 4, B_HEADS).astype(F32) + gate_b.astype(F32)
        pre = pre.transpose(2, 0, 3, 1)
        base = (th(q), th(k) * B_DH ** -0.5, th(v))
        fwd = base + (pre[0], jax.nn.log_sigmoid(pre[1]))
        bwd = base + (pre[2], jax.nn.log_sigmoid(pre[3]))
        return fwd, bwd, o

    lat_f, lat_b, o_l = prep(*pl)
    ctx_f, ctx_b, o_c = prep(*pc)
    bsz = o_l.shape[0]
    init = (jnp.zeros((bsz, B_HEADS, B_DH, B_DH), F32), jnp.zeros((bsz, B_HEADS, B_DH), F32),
            jnp.zeros((bsz, B_HEADS), F32))
    hl, hc = bidir(_mlstm_step, init, ctx_f, lat_f, ctx_b, lat_b)

    def finish(h, o):
        h = rmsnorm(jnp.swapaxes(h, 1, 2), outn_g)
        return h.reshape(h.shape[0], h.shape[1], B_HEADS * B_DH) * jax.nn.sigmoid(o.astype(F32))

    out_c = finish(hc, o_c) if with_ctx else None
    return finish(hl, o_l), out_c


def gla_mixer(pl, pc, wg, bg, outn_g, with_ctx):
    def prep(q, k, v, lr):
        bsz, t = q.shape[:2]
        th = lambda a, dd: a.reshape(bsz, t, C_HEADS, dd).transpose(0, 2, 1, 3).astype(F32)
        lr = lr.reshape(bsz, t, 2, GLA_RANK).astype(F32)
        la = jax.nn.log_sigmoid(jnp.einsum('btzr,zrk->btzk', lr, wg.astype(F32)) + bg.astype(F32)) / GLA_TAU
        base = (th(q, C_DK) * C_DK ** -0.5, th(k, C_DK), th(v, C_DV))
        return base + (th(la[:, :, 0], C_DK),), base + (th(la[:, :, 1], C_DK),)

    lat_f, lat_b = prep(*pl)
    ctx_f, ctx_b = prep(*pc)
    bsz = pl[0].shape[0]
    init = jnp.zeros((bsz, C_HEADS, C_DK, C_DV), F32)
    hl, hc = bidir(_gla_step, init, ctx_f, lat_f, ctx_b, lat_b)

    def finish(h):
        h = rmsnorm(jnp.swapaxes(h, 1, 2), outn_g)
        return h.reshape(h.shape[0], h.shape[1], C_HEADS * C_DV)

    out_c = finish(hc) if with_ctx else None
    return finish(hl), out_c


def window_gqa(pl, pc, qn_g, kn_g, sink, rows, cols, with_ctx):
    (ql, kl, vl), (qc, kc, vc) = pl, pc
    bsz, t = ql.shape[:2]
    lc = qc.shape[1]
    ql = axial_rope(rmsnorm(ql.reshape(bsz, t, D_KV, D_GROUP, D_HD), qn_g), rows, cols)
    kl = axial_rope(rmsnorm(kl.reshape(bsz, t, D_KV, D_HD), kn_g), rows, cols)
    vl = vl.reshape(bsz, t, D_KV, D_HD)
    qc = rmsnorm(qc.reshape(bsz, lc, D_KV, D_GROUP, D_HD), qn_g)
    kc = rmsnorm(kc.reshape(bsz, lc, D_KV, D_HD), kn_g)
    vc = vc.reshape(bsz, lc, D_KV, D_HD)
    scale = D_HD ** -0.5
    sk = sink.astype(F32).reshape(D_KV, D_GROUP, 1, 1)
    nb = t // BLOCK
    pad = ((0, 0), (WINDOW, WINDOW), (0, 0), (0, 0))
    kb = jnp.pad(kl, pad).reshape(bsz, nb + 2, BLOCK, D_KV, D_HD)
    vb = jnp.pad(vl, pad).reshape(bsz, nb + 2, BLOCK, D_KV, D_HD)
    kw = jnp.concatenate([kb[:, :-2], kb[:, 1:-1], kb[:, 2:]], axis=2)
    vw = jnp.concatenate([vb[:, :-2], vb[:, 1:-1], vb[:, 2:]], axis=2)
    qb = ql.reshape(bsz, nb, BLOCK, D_KV, D_GROUP, D_HD)
    s_loc = jnp.einsum('bnqhgd,bnkhd->bnhgqk', qb, kw).astype(F32) * scale
    a_idx = jnp.arange(BLOCK)[:, None]
    j_idx = jnp.arange(3 * BLOCK)[None, :]
    kpos = jnp.arange(nb)[:, None, None] * BLOCK - WINDOW + j_idx[None]
    band = (j_idx - a_idx >= 0) & (j_idx - a_idx <= 2 * WINDOW)
    valid = band[None] & (kpos >= 0) & (kpos < t)
    s_loc = jnp.where(valid[None, :, None, None], s_loc, -jnp.inf)
    s_ctx = jnp.einsum('bnqhgd,bchd->bnhgqc', qb, kc).astype(F32) * scale
    logits = jnp.concatenate([s_loc, s_ctx, jnp.broadcast_to(sk, s_loc.shape[:-1] + (1,))], axis=-1)
    p = jax.nn.softmax(logits, axis=-1)
    o = (jnp.einsum('bnhgqk,bnkhd->bnqhgd', p[..., :3 * BLOCK], vw.astype(F32))
         + jnp.einsum('bnhgqc,bchd->bnqhgd', p[..., 3 * BLOCK:3 * BLOCK + lc], vc.astype(F32)))
    out_l = o.reshape(bsz, t, D_HEADS * D_HD)
    out_c = None
    if with_ctx:
        s = jnp.einsum('bqhgd,bkhd->bhgqk', qc, kc).astype(F32) * scale
        lg = jnp.concatenate([s, jnp.broadcast_to(sk, s.shape[:-1] + (1,))], axis=-1)
        pc_ = jax.nn.softmax(lg, axis=-1)
        out_c = jnp.einsum('bhgqk,bkhd->bqhgd', pc_[..., :lc], vc.astype(F32)).reshape(bsz, lc, D_HEADS * D_HD)
    return out_l, out_c


def setup_inputs(seed: int = 0) -> dict:
    key = jax.random.key(seed)
    ks = jax.random.split(key, 24)
    nrm = lambda k, shape, s: s * jax.random.normal(k, shape, F32)
    gain = lambda k, shape: 1.0 + 0.1 * jax.random.normal(k, shape, F32)
    f_bias = jnp.linspace(3.0, 6.0, B_HEADS, dtype=F32)
    zb = jnp.zeros((B_HEADS,), F32)
    gate_base = jnp.stack([zb, f_bias, zb, f_bias])
    return {
        'x': nrm(ks[0], (BATCH, SEQ, D_MODEL), 1.0),
        'c': nrm(ks[1], (BATCH, D_MODEL), 1.0),
        'ctx': nrm(ks[2], (BATCH, CTX_LEN, D_MODEL), 1.0),
        'c_ctx': nrm(ks[3], (D_MODEL,), 1.0),
        'w_mod': nrm(ks[4], (DEPTH, D_MODEL, 3 * D_MODEL), 0.5 * D_MODEL ** -0.5),
        'b_mod': nrm(ks[5], (DEPTH, 3 * D_MODEL), 0.02),
        'norm_g': gain(ks[6], (DEPTH, D_MODEL)),
        'w_in': nrm(ks[7], (DEPTH, D_MODEL, N_IN), D_MODEL ** -0.5),
        'w_out': nrm(ks[8], (DEPTH, D_MIX, D_MODEL), D_MIX ** -0.5),
        'a_qn': gain(ks[9], (DEPTH, A_DK)),
        'a_kn': gain(ks[10], (DEPTH, A_DK)),
        'a_lam': nrm(ks[11], (DEPTH, 4, A_DK), 0.1),
        'a_subln': gain(ks[12], (DEPTH, A_DV)),
        'b_conv_w': nrm(ks[13], (DEPTH, CONV_W, 2 * B_HEADS * B_DH), CONV_W ** -0.5),
        'b_conv_b': nrm(ks[14], (DEPTH, 2 * B_HEADS * B_DH), 0.02),
        'b_gate_b': gate_base + nrm(ks[15], (DEPTH, 4, B_HEADS), 0.1),
        'b_outn': gain(ks[16], (DEPTH, B_DH)),
        'c_wg': nrm(ks[17], (DEPTH, 2, GLA_RANK, C_HEADS * C_DK), GLA_RANK ** -0.5),
        'c_bg': nrm(ks[18], (DEPTH, 2, C_HEADS * C_DK), 0.1),
        'c_outn': gain(ks[19], (DEPTH, C_DV)),
        'd_qn': gain(ks[20], (DEPTH, D_HD)),
        'd_kn': gain(ks[21], (DEPTH, D_HD)),
        'd_sink': nrm(ks[22], (DEPTH, D_HEADS), 0.5),
    }


def reference(x, c, ctx, c_ctx, w_mod, b_mod, norm_g, w_in, w_out, a_qn, a_kn, a_lam, a_subln,
              b_conv_w, b_conv_b, b_gate_b, b_outn, c_wg, c_bg, c_outn, d_qn, d_kn, d_sink):
    t = x.shape[1]
    rows_n = t // GRID_W
    rows = jnp.repeat(jnp.arange(rows_n, dtype=jnp.int32), GRID_W)
    cols = jnp.tile(jnp.arange(GRID_W, dtype=jnp.int32), rows_n)
    s_lat = jax.nn.silu(c)
    s_ctx = jax.nn.silu(c_ctx)
    for l in range(DEPTH):
        with_ctx = l < DEPTH - 1
        sh, sc, gt = jnp.split(s_lat @ w_mod[l] + b_mod[l], 3, axis=-1)
        sh_c, sc_c, gt_c = jnp.split(s_ctx @ w_mod[l] + b_mod[l], 3, axis=-1)
        hx = rmsnorm(x, norm_g[l]) * (1.0 + sc[:, None]) + sh[:, None]
        hc = rmsnorm(ctx, norm_g[l]) * (1.0 + sc_c) + sh_c
        px = split_cols(hx @ w_in[l])
        pcx = split_cols(hc @ w_in[l])
        lam_init = 0.8 - 0.6 * math.exp(-0.3 * l)
        a_l, a_c = diff_attention(px[0:3], pcx[0:3], a_qn[l], a_kn[l], a_lam[l], a_subln[l], lam_init,
                                  rows, cols, with_ctx)
        b_l, b_c = mlstm_mixer(px[3:8], pcx[3:8], b_conv_w[l], b_conv_b[l], b_gate_b[l], b_outn[l], with_ctx)
        c_l, c_c = gla_mixer(px[8:12], pcx[8:12], c_wg[l], c_bg[l], c_outn[l], with_ctx)
        d_l, d_c = window_gqa(px[12:15], pcx[12:15], d_qn[l], d_kn[l], d_sink[l], rows, cols, with_ctx)
        y = jnp.concatenate([a_l, b_l, c_l, d_l], axis=-1).astype(x.dtype) * jax.nn.silu(px[15])
        x = x + gt[:, None] * (y @ w_out[l])
        if with_ctx:
            yc = jnp.concatenate([a_c, b_c, c_c, d_c], axis=-1).astype(ctx.dtype) * jax.nn.silu(pcx[15])
            ctx = ctx + gt_c * (yc @ w_out[l])
    return x
```

```python
import math
from functools import partial

import jax
import jax.numpy as jnp
from jax import lax
from jax.experimental import pallas as pl
from jax.experimental.pallas import tpu as pltpu

F32 = jnp.float32
BF16 = jnp.bfloat16
HI = lax.Precision.HIGHEST

D_MODEL = 1024
CTX = 256
GRID_W = 64
GROUP_W = 256
N_HEADS = 4
A_DK = 32
A_DV = 64
B_DH = 64
C_DK = 32
C_DV = 64
GLA_RANK = 16
GLA_TAU = 16.0
D_HD = 64
WINDOW = 128
ROPE_THETA = 10000.0
EPS = 1e-6

LANES = 128
VMEM_LIMIT = 56 * 1024 * 1024

PA_W = 768
PB_W = 1152
PC_W = 640
PD_W = 512
PG_W = 1024
P_W = PA_W + PB_W + PC_W + PD_W + PG_W

ROW_TILE = 256
SCAN_CHUNK = 128
GLA_SUB = 16
ATT_TQ = 256
ATT_TK = 256

NEG_BIG = -1e30


def _cparams(sem):
    return pltpu.CompilerParams(dimension_semantics=sem, vmem_limit_bytes=VMEM_LIMIT)


def _log_sigmoid(x):
    return jnp.minimum(x, 0.0) - jnp.log(1.0 + jnp.exp(-jnp.abs(x)))


def _sigmoid(x):
    return 1.0 / (1.0 + jnp.exp(-x))


def _lane_iota(shape):
    return lax.broadcasted_iota(jnp.int32, shape, len(shape) - 1)


def _head_mask(width, per_head, h, rows=1):
    lane = _lane_iota((rows, width))
    return (lane >= h * per_head) & (lane < (h + 1) * per_head)


def _group_matrix(n_rows, rows_per_group, n_cols, cols_per_group, value=1.0):
    r = lax.broadcasted_iota(jnp.int32, (n_rows, n_cols), 0)
    c = lax.broadcasted_iota(jnp.int32, (n_rows, n_cols), 1)
    same = jnp.zeros((n_rows, n_cols), jnp.bool_)
    for g in range(n_rows // rows_per_group):
        same = same | ((r >= g * rows_per_group) & (r < (g + 1) * rows_per_group)
                       & (c >= g * cols_per_group) & (c < (g + 1) * cols_per_group))
    return jnp.where(same, value, 0.0).astype(F32)


def _dot(a, b, precision=None):
    return jnp.dot(a, b, preferred_element_type=F32, precision=precision)


def _dot_nt(a, b, precision=None):
    return lax.dot_general(a, b, (((1,), (1,)), ((), ())), preferred_element_type=F32,
                           precision=precision)


def _mod_kernel(s_ref, w_ref, b_ref, o_ref):
    s = s_ref[...]
    s = s * _sigmoid(s)
    o_ref[0] = _dot(s, w_ref[0], HI) + b_ref[0]


def _modulation(s_in, w_mod, b_mod):
    depth, d, n = w_mod.shape
    tn = 1024
    return pl.pallas_call(
        _mod_kernel,
        grid=(depth, n // tn),
        in_specs=[pl.BlockSpec((8, d), lambda l, j: (0, 0)),
                  pl.BlockSpec((1, d, tn), lambda l, j: (l, 0, j)),
                  pl.BlockSpec((1, 1, tn), lambda l, j: (l, 0, j))],
        out_specs=pl.BlockSpec((1, 8, tn), lambda l, j: (l, 0, j)),
        out_shape=jax.ShapeDtypeStruct((depth, 8, n), F32),
        compiler_params=_cparams(("arbitrary", "arbitrary")),
        name="modulation",
    )(s_in, w_mod, b_mod.reshape(depth, 1, n))


def _proj_kernel(x_ref, mod_ref, g_ref, w_ref, pa_ref, pb_ref, pc_ref, pd_ref, pg_ref):
    x = x_ref[0]
    ms = jnp.mean(x * x, axis=-1, keepdims=True)
    y = x * lax.rsqrt(ms + EPS) * g_ref[...]
    mod = mod_ref[0, 0]
    hx = y * (1.0 + mod[1:2]) + mod[0:1]
    p = _dot(hx.astype(BF16), w_ref[...])
    o = 0
    for ref, w in ((pa_ref, PA_W), (pb_ref, PB_W), (pc_ref, PC_W), (pd_ref, PD_W), (pg_ref, PG_W)):
        ref[0] = p[:, o:o + w]
        o += w


def _in_projection(xc, msel, norm_g, w_in_p):
    bsz, r, d = xc.shape
    tm = ROW_TILE
    nct = CTX // tm
    row = lambda w: pl.BlockSpec((1, tm, w), lambda b, i: (b, i, 0))
    return pl.pallas_call(
        _proj_kernel,
        grid=(bsz, r // tm),
        in_specs=[row(d),
                  pl.BlockSpec((1, 1, 3, d), lambda b, i: (b, jnp.where(i >= nct, 1, 0), 0, 0)),
                  pl.BlockSpec((1, d), lambda b, i: (0, 0)),
                  pl.BlockSpec((d, P_W), lambda b, i: (0, 0))],
        out_specs=[row(PA_W), row(PB_W), row(PC_W), row(PD_W), row(PG_W)],
        out_shape=[jax.ShapeDtypeStruct((bsz, r, w), F32) for w in (PA_W, PB_W, PC_W, PD_W, PG_W)],
        compiler_params=_cparams(("arbitrary", "arbitrary")),
        name="in_projection",
    )(xc, msel, norm_g.reshape(1, d), w_in_p)


def _norm_rope(x, gain, cos, sin, group, half_pair, scale):
    w = x.shape[-1]
    gmat = _group_matrix(w, group, w, group, 1.0 / group)
    ms = _dot(x * x, gmat, HI)
    xn = x * lax.rsqrt(ms + EPS) * gain
    lane = _lane_iota(xn.shape)
    first = (lane & (2 * half_pair - 1)) < half_pair
    partner = jnp.where(first, pltpu.roll(xn, w - half_pair, 1), pltpu.roll(xn, half_pair, 1))
    out = xn * cos + partner * sin
    return out * scale if scale != 1.0 else out


def _prep_attn_kernel(pa_ref, pd_ref, cosa_ref, sina_ref, cosd_ref, sind_ref, gains_ref,
                      qa_ref, kta_ref, va_ref, qd_ref, kd_ref, vd_ref):
    pa = pa_ref[0]
    cosa, sina = cosa_ref[...], sina_ref[...]
    gains = gains_ref[...]
    qa = _norm_rope(pa[:, 0:256], gains[0:1], cosa, sina, A_DK, A_DK // 4, A_DK ** -0.5)
    ka = _norm_rope(pa[:, 256:512], gains[1:2], cosa, sina, A_DK, A_DK // 4, 1.0)
    qa_ref[0] = qa.astype(BF16)
    kta_ref[0] = ka.T.astype(BF16)
    va = pa[:, 512:768]
    rows = va.shape[0]
    ones_col = jnp.where(_lane_iota((rows, A_DV)) == 0, 1.0, 0.0).astype(F32)
    for h in range(N_HEADS):
        va_ref[0, h] = jnp.concatenate([va[:, h * A_DV:(h + 1) * A_DV], ones_col], axis=-1).astype(BF16)
    pd = pd_ref[0]
    cosd, sind = cosd_ref[...], sind_ref[...]
    qd = _norm_rope(pd[:, 0:256], gains[2:3], cosd, sind, D_HD, D_HD // 4, D_HD ** -0.5)
    kd = _norm_rope(pd[:, 256:384], gains[3:4, 0:128], cosd[:, 0:128], sind[:, 0:128], D_HD, D_HD // 4, 1.0)
    qd_ref[0] = qd.astype(BF16)
    kd_ref[0] = kd.astype(BF16)
    vd_ref[0] = pd[:, 384:512].astype(BF16)


def _prep_attn(pa, pd, tabs, gains):
    bsz, r, _ = pa.shape
    tr = ROW_TILE
    row = lambda w: pl.BlockSpec((1, tr, w), lambda b, i: (b, i, 0))
    tab = pl.BlockSpec((tr, 256), lambda b, i: (i, 0))
    return pl.pallas_call(
        _prep_attn_kernel,
        grid=(bsz, r // tr),
        in_specs=[row(PA_W), row(PD_W), tab, tab, tab, tab, pl.BlockSpec((8, 256), lambda b, i: (0, 0))],
        out_specs=[row(256),
                   pl.BlockSpec((1, 256, tr), lambda b, i: (b, 0, i)),
                   pl.BlockSpec((1, N_HEADS, tr, LANES), lambda b, i: (b, 0, i, 0)),
                   row(256), row(128), row(128)],
        out_shape=[jax.ShapeDtypeStruct((bsz, r, 256), BF16),
                   jax.ShapeDtypeStruct((bsz, 256, r), BF16),
                   jax.ShapeDtypeStruct((bsz, N_HEADS, r, LANES), BF16),
                   jax.ShapeDtypeStruct((bsz, r, 256), BF16),
                   jax.ShapeDtypeStruct((bsz, r, 128), BF16),
                   jax.ShapeDtypeStruct((bsz, r, 128), BF16)],
        compiler_params=_cparams(("arbitrary", "arbitrary")),
        name="prep_attn",
    )(pa, pd, *tabs, gains)


def _diff_attn_kernel(q_ref, kt_ref, v_ref, lam_ref, g_ref, o_ref, *, first_block, lam_init):
    i = pl.program_id(1) + first_block
    n_all = kt_ref.shape[2] // ATT_TK
    nkt = jnp.where(i == 0, CTX // ATT_TK, n_all)
    lp = lam_ref[...]
    lam = (jnp.exp(jnp.sum(lp[0:1] * lp[1:2], axis=-1, keepdims=True))
           - jnp.exp(jnp.sum(lp[2:3] * lp[3:4], axis=-1, keepdims=True)) + lam_init)
    q_all = q_ref[0]
    tq = q_all.shape[0]
    outs = []
    for h in range(N_HEADS):
        comp = []
        for c in range(2):
            r0 = h * 2 * A_DK + c * A_DK
            q = q_all[:, r0:r0 + A_DK]

            def step(j, carry, r0=r0, q=q, h=h):
                m, acc = carry
                off = pl.multiple_of(j * ATT_TK, ATT_TK)
                kt = kt_ref[0, r0:r0 + A_DK, pl.ds(off, ATT_TK)]
                s = _dot(q, kt)
                m_new = jnp.maximum(m, jnp.max(s, axis=-1, keepdims=True))
                alpha = jnp.exp(m - m_new)
                p = jnp.exp(s - m_new)
                acc = alpha * acc + _dot(p.astype(BF16), v_ref[0, h, pl.ds(off, ATT_TK), :])
                return m_new, acc

            m0 = jnp.full((tq, 1), NEG_BIG, F32)
            acc0 = jnp.zeros((tq, LANES), F32)
            _, acc = lax.fori_loop(0, nkt, step, (m0, acc0))
            comp.append(acc[:, 0:A_DV] / acc[:, A_DV:A_DV + 1])
        o = comp[0] - lam * comp[1]
        ms = jnp.mean(o * o, axis=-1, keepdims=True)
        outs.append(o * lax.rsqrt(ms + EPS) * g_ref[...] * (1.0 - lam_init))
    o_ref[0] = jnp.concatenate(outs, axis=-1)


def _diff_attention(qa, kta, va, a_lam, a_subln, lam_init, with_ctx):
    bsz, r, _ = qa.shape
    first = 0 if with_ctx else CTX // ATT_TQ
    nq = r // ATT_TQ - first
    return pl.pallas_call(
        partial(_diff_attn_kernel, first_block=first, lam_init=lam_init),
        grid=(bsz, nq),
        in_specs=[pl.BlockSpec((1, ATT_TQ, 256), lambda b, i: (b, i + first, 0)),
                  pl.BlockSpec((1, 256, r), lambda b, i: (b, 0, 0)),
                  pl.BlockSpec((1, N_HEADS, r, LANES), lambda b, i: (b, 0, 0, 0)),
                  pl.BlockSpec((4, A_DK), lambda b, i: (0, 0)),
                  pl.BlockSpec((1, A_DV), lambda b, i: (0, 0))],
        out_specs=pl.BlockSpec((1, ATT_TQ, 256), lambda b, i: (b, i + first, 0)),
        out_shape=jax.ShapeDtypeStruct((bsz, r, 256), F32),
        compiler_params=_cparams(("arbitrary", "arbitrary")),
        name="diff_attention",
    )(qa, kta, va, a_lam, a_subln.reshape(1, A_DV))


def _window_attn_kernel(q_ref, k_ref, v_ref, sink_ref, o_ref, *, first_block):
    i = pl.program_id(1) + first_block
    tq = q_ref.shape[1]
    r = k_ref.shape[1]
    span = tq + 2 * WINDOW
    start = jnp.clip(i * tq - WINDOW, CTX, r - span)
    start = pl.multiple_of(start, WINDOW)
    kwin = k_ref[0, pl.ds(start, span), :]
    vwin = v_ref[0, pl.ds(start, span), :]
    kctx = k_ref[0, 0:CTX, :]
    vctx = v_ref[0, 0:CTX, :]
    qpos = i * tq + lax.broadcasted_iota(jnp.int32, (tq, span), 0)
    kpos = start + lax.broadcasted_iota(jnp.int32, (tq, span), 1)
    valid = (jnp.abs(kpos - qpos) <= WINDOW) & (qpos >= CTX)
    q_all = q_ref[0]
    sink = sink_ref[...]
    outs = []
    for h in range(N_HEADS):
        kv = h // 2
        q = q_all[:, h * D_HD:(h + 1) * D_HD]
        s_loc = _dot_nt(q, kwin[:, kv * D_HD:(kv + 1) * D_HD])
        s_loc = jnp.where(valid, s_loc, -jnp.inf)
        s_ctx = _dot_nt(q, kctx[:, kv * D_HD:(kv + 1) * D_HD])
        sk = sink[0:1, h:h + 1]
        m = jnp.maximum(jnp.maximum(jnp.max(s_loc, axis=-1, keepdims=True),
                                    jnp.max(s_ctx, axis=-1, keepdims=True)), sk)
        p_loc = jnp.exp(s_loc - m)
        p_ctx = jnp.exp(s_ctx - m)
        den = (jnp.sum(p_loc, axis=-1, keepdims=True) + jnp.sum(p_ctx, axis=-1, keepdims=True)
               + jnp.exp(sk - m))
        o = (_dot(p_loc.astype(BF16), vwin[:, kv * D_HD:(kv + 1) * D_HD])
             + _dot(p_ctx.astype(BF16), vctx[:, kv * D_HD:(kv + 1) * D_HD]))
        outs.append(o / den)
    o_ref[0] = jnp.concatenate(outs, axis=-1)


def _window_attention(qd, kd, vd, sink, with_ctx):
    bsz, r, _ = qd.shape
    tq = ROW_TILE
    first = 0 if with_ctx else CTX // tq
    return pl.pallas_call(
        partial(_window_attn_kernel, first_block=first),
        grid=(bsz, r // tq - first),
        in_specs=[pl.BlockSpec((1, tq, 256), lambda b, i: (b, i + first, 0)),
                  pl.BlockSpec((1, r, 128), lambda b, i: (b, 0, 0)),
                  pl.BlockSpec((1, r, 128), lambda b, i: (b, 0, 0)),
                  pl.BlockSpec((1, LANES), lambda b, i: (0, 0))],
        out_specs=pl.BlockSpec((1, tq, 256), lambda b, i: (b, i + first, 0)),
        out_shape=jax.ShapeDtypeStruct((bsz, r, 256), F32),
        compiler_params=_cparams(("arbitrary", "arbitrary")),
        name="window_attention",
    )(qd, kd, vd, sink)


def _chunk_of_step(i, n_chunks, n_ctx_chunks, rev):
    if not rev:
        return i
    return jnp.where(i < n_ctx_chunks, n_ctx_chunks - 1 - i, n_chunks - 1 - (i - n_ctx_chunks))


def _tri(n, rev):
    t = lax.broadcasted_iota(jnp.int32, (n, n), 0)
    s = lax.broadcasted_iota(jnp.int32, (n, n), 1)
    return jnp.where((s >= t) if rev else (s <= t), 1.0, 0.0).astype(F32)


def _mlstm_kernel(p_ref, prev_ref, next_ref, cw_ref, cb_ref, gb_ref, o_ref, c_scr, n_scr, m_scr,
                  *, rev, n_chunks, n_ctx_chunks):
    i = pl.program_id(1)
    chunk = _chunk_of_step(i, n_chunks, n_ctx_chunks, rev)
    ln = p_ref.shape[1]
    hw = N_HEADS * B_DH

    @pl.when(i == 0)
    def _():
        c_scr[...] = jnp.zeros_like(c_scr)
        n_scr[...] = jnp.zeros_like(n_scr)
        m_scr[...] = jnp.zeros_like(m_scr)

    x = p_ref[0, :, 0:2 * hw]
    seg_first = (chunk == 0) | (chunk == n_ctx_chunks)
    seg_last = (chunk == n_ctx_chunks - 1) | (chunk == n_chunks - 1)
    row_prev = jnp.where(seg_first, 0.0, prev_ref[0, 7:8, :])
    row_next = jnp.where(seg_last, 0.0, next_ref[0, 0:1, :])
    ridx = lax.broadcasted_iota(jnp.int32, x.shape, 0)
    x_prev = jnp.where(ridx == 0, row_prev, pltpu.roll(x, 1, 0))
    x_next = jnp.where(ridx == ln - 1, row_next, pltpu.roll(x, ln - 1, 0))
    cw = cw_ref[...]
    y = cw[0:1] * x_prev + cw[1:2] * x + cw[2:3] * x_next + cb_ref[...]
    y = y * _sigmoid(y)
    q = y[:, 0:hw]
    k = y[:, hw:2 * hw] * (B_DH ** -0.5)
    v = p_ref[0, :, 2 * hw:3 * hw]

    g = p_ref[0, :, 4 * hw:4 * hw + LANES] + gb_ref[...]
    gt = g.T
    base = 8 if rev else 0
    tri = _tri(ln, rev)
    lf_c = _log_sigmoid(g)
    lf_r = _log_sigmoid(gt)
    b_c = _dot(tri, lf_c, HI)
    b_r = _dot_nt(lf_r, tri, HI)
    end = 0 if rev else ln - 1
    t_idx = lax.broadcasted_iota(jnp.int32, (ln, ln), 0)
    s_idx = lax.broadcasted_iota(jnp.int32, (ln, ln), 1)
    causal = (s_idx >= t_idx) if rev else (s_idx <= t_idx)

    qb = q.astype(BF16)
    kb = k.astype(BF16)
    vb = v.astype(BF16)
    c_old = c_scr[...]
    n_old = n_scr[0:1, :]
    m_old = m_scr[0:1, :]
    num_inter = _dot(qb, c_old.astype(BF16))
    seg = _group_matrix(hw, B_DH, LANES, 1)
    nq_inter = _dot(q * n_old, seg, HI)

    h_out = jnp.zeros((ln, hw), F32)
    w_end_full = jnp.zeros((ln, hw), F32)
    decay_row = jnp.zeros((1, hw), F32)
    m_new_row = jnp.zeros((1, LANES), F32)
    lane128 = _lane_iota((1, LANES))
    for h in range(N_HEADS):
        hm = _head_mask(hw, B_DH, h)
        bc_h = b_c[:, base + 4 + h:base + 5 + h]
        br_h = b_r[base + 4 + h:base + 5 + h, :]
        li_c = g[:, base + h:base + h + 1]
        li_r = gt[base + h:base + h + 1, :]
        m_h = m_old[:, h:h + 1]
        d = jnp.where(causal, bc_h - br_h + li_r, -jnp.inf)
        m_inter = bc_h + m_h
        m_t = jnp.maximum(m_inter, jnp.max(d, axis=-1, keepdims=True))
        w = jnp.exp(d - m_t)
        g_inter = jnp.exp(m_inter - m_t)
        s = _dot_nt(jnp.where(hm, q, 0.0).astype(BF16), kb) * w
        num = _dot(s.astype(BF16), vb) + g_inter * num_inter
        nq = jnp.sum(s, axis=-1, keepdims=True) + g_inter * nq_inter[:, h:h + 1]
        den = jnp.maximum(jnp.abs(nq), jnp.exp(-m_t))
        h_out = jnp.where(hm, num / den, h_out)
        b_end = bc_h[end:end + 1, :]
        g_c = b_end - bc_h + li_c
        g_r = b_end - br_h + li_r
        m_new = jnp.maximum(b_end + m_h, jnp.max(g_r, axis=-1, keepdims=True))
        w_end_full = jnp.where(hm, jnp.exp(g_c - m_new), w_end_full)
        decay_row = jnp.where(hm, jnp.exp(b_end + m_h - m_new), decay_row)
        m_new_row = jnp.where(lane128 == h, m_new, m_new_row)
    o_ref[0] = h_out

    kw = k * w_end_full
    upd = _dot(kw.T.astype(BF16), vb)
    blk = _group_matrix(hw, B_DH, hw, B_DH)
    decay_col = _dot_nt(blk, decay_row, HI)[:, 0:1] * (1.0 / B_DH)
    c_scr[...] = decay_col * c_old + blk * upd
    n_scr[0:1, :] = decay_row * n_old + jnp.sum(kw, axis=0, keepdims=True)
    m_scr[0:1, :] = m_new_row


def _mlstm(pb, conv_w, conv_b, gate_b, rev):
    bsz, r, _ = pb.shape
    ln = SCAN_CHUNK
    nc, ncc = r // ln, CTX // ln
    hw = N_HEADS * B_DH
    chunk = lambda i: _chunk_of_step(i, nc, ncc, rev)
    per8 = ln // 8
    return pl.pallas_call(
        partial(_mlstm_kernel, rev=rev, n_chunks=nc, n_ctx_chunks=ncc),
        grid=(bsz, nc),
        in_specs=[pl.BlockSpec((1, ln, PB_W), lambda b, i: (b, chunk(i), 0)),
                  pl.BlockSpec((1, 8, 2 * hw), lambda b, i: (b, jnp.maximum(chunk(i) * per8 - 1, 0), 0)),
                  pl.BlockSpec((1, 8, 2 * hw), lambda b, i: (b, jnp.minimum((chunk(i) + 1) * per8, r // 8 - 1), 0)),
                  pl.BlockSpec((3, 2 * hw), lambda b, i: (0, 0)),
                  pl.BlockSpec((1, 2 * hw), lambda b, i: (0, 0)),
                  pl.BlockSpec((1, LANES), lambda b, i: (0, 0))],
        out_specs=pl.BlockSpec((1, ln, hw), lambda b, i: (b, chunk(i), 0)),
        out_shape=jax.ShapeDtypeStruct((bsz, r, hw), F32),
        scratch_shapes=[pltpu.VMEM((hw, hw), F32), pltpu.VMEM((8, hw), F32), pltpu.VMEM((8, LANES), F32)],
        compiler_params=_cparams(("arbitrary", "arbitrary")),
        name="mlstm_bwd" if rev else "mlstm_fwd",
    )(pb, pb, pb, conv_w, conv_b, gate_b)


def _gla_kernel(p_ref, wg_ref, bg_ref, o_ref, s_scr, *, rev, n_chunks, n_ctx_chunks):
    i = pl.program_id(1)
    ln = p_ref.shape[1]
    kw_ = N_HEADS * C_DK
    vw_ = N_HEADS * C_DV
    c = GLA_SUB
    nb = ln // c

    @pl.when(i == 0)
    def _():
        s_scr[...] = jnp.zeros_like(s_scr)

    q = p_ref[0, :, 0:kw_] * (C_DK ** -0.5)
    k = p_ref[0, :, kw_:2 * kw_]
    v = p_ref[0, :, 2 * kw_:2 * kw_ + vw_]
    lr = p_ref[0, :, 2 * kw_ + vw_:2 * kw_ + vw_ + LANES]
    la = _log_sigmoid(_dot(lr, wg_ref[...], HI) + bg_ref[...]) * (1.0 / GLA_TAU)
    bc = _dot(_tri(ln, rev), la, HI)
    vb = v.astype(BF16)
    blk = _group_matrix(kw_, C_DK, vw_, C_DV)
    blk_b = blk.astype(BF16)

    s_old = s_scr[...]
    o_acc = _dot((q * jnp.exp(bc)).astype(BF16), s_old.astype(BF16))

    t_loc = lax.broadcasted_iota(jnp.int32, (c, kw_), 0)
    diag_parts = []
    for b in range(nb):
        sl = slice(b * c, (b + 1) * c)
        bcb, qb_, kb_, vb_ = bc[sl], q[sl], k[sl], v[sl]
        pieces = []
        for t in range(c):
            keep = (t_loc >= t) if rev else (t_loc <= t)
            diff = jnp.minimum(bcb[t:t + 1] - bcb, 0.0)
            pieces.append(jnp.where(keep, qb_[t:t + 1] * kb_ * jnp.exp(diff), 0.0))
        e = jnp.concatenate(pieces, axis=0)
        a = _dot(e.astype(BF16), blk_b)
        a = a.reshape(c, c, vw_) * vb_[None, :, :]
        diag_parts.append(jnp.sum(a, axis=1))
    o_acc = o_acc + jnp.concatenate(diag_parts, axis=0)

    rmask = lax.broadcasted_iota(jnp.int32, (N_HEADS * c, kw_), 0)
    lmask = _lane_iota((N_HEADS * c, kw_))
    rmask_v = lax.broadcasted_iota(jnp.int32, (N_HEADS * c, vw_), 0)
    lmask_v = _lane_iota((N_HEADS * c, vw_))
    same_k = jnp.zeros((N_HEADS * c, kw_), jnp.bool_)
    same_v = jnp.zeros((N_HEADS * c, vw_), jnp.bool_)
    for h in range(N_HEADS):
        same_k = same_k | ((rmask >= h * c) & (rmask < (h + 1) * c) & (lmask >= h * C_DK) & (lmask < (h + 1) * C_DK))
        same_v = same_v | ((rmask_v >= h * c) & (rmask_v < (h + 1) * c) & (lmask_v >= h * C_DV) & (lmask_v < (h + 1) * C_DV))
    off_parts = [jnp.zeros((c, vw_), F32) for _ in range(nb)]
    off_full = jnp.zeros((ln, vw_), F32)
    for j in range(nb):
        if rev:
            if j == 0:
                continue
            rows = slice(0, j * c)
            edge = bc[j * c:j * c + 1]
        else:
            if j == nb - 1:
                continue
            rows = slice((j + 1) * c, ln)
            edge = bc[(j + 1) * c - 1:(j + 1) * c]
        sl = slice(j * c, (j + 1) * c)
        qs = q[rows] * jnp.exp(bc[rows] - edge)
        ks = k[sl] * jnp.exp(edge - bc[sl])
        kbd = jnp.where(same_k, jnp.concatenate([ks] * N_HEADS, axis=0), 0.0)
        vbd = jnp.where(same_v, jnp.concatenate([v[sl]] * N_HEADS, axis=0), 0.0)
        a = _dot_nt(qs.astype(BF16), kbd.astype(BF16))
        contrib = _dot(a.astype(BF16), vbd.astype(BF16))
        n_rows = contrib.shape[0]
        pad = jnp.zeros((ln - n_rows, vw_), F32)
        off_full = off_full + (jnp.concatenate([contrib, pad], axis=0) if rev
                               else jnp.concatenate([pad, contrib], axis=0))
    del off_parts
    o_ref[0] = o_acc + off_full

    end = 0 if rev else ln - 1
    b_end = bc[end:end + 1]
    kd = k * jnp.exp(b_end - bc)
    upd = _dot(kd.T.astype(BF16), vb)
    eye = jnp.where(lax.broadcasted_iota(jnp.int32, (kw_, kw_), 0)
                    == lax.broadcasted_iota(jnp.int32, (kw_, kw_), 1), 1.0, 0.0).astype(F32)
    decay_col = jnp.sum(eye * jnp.exp(b_end), axis=-1, keepdims=True)
    s_scr[...] = decay_col * s_old + blk * upd


def _gla(pc, wg_pad, bg, rev):
    bsz, r, _ = pc.shape
    ln = SCAN_CHUNK
    nc, ncc = r // ln, CTX // ln
    chunk = lambda i: _chunk_of_step(i, nc, ncc, rev)
    return pl.pallas_call(
        partial(_gla_kernel, rev=rev, n_chunks=nc, n_ctx_chunks=ncc),
        grid=(bsz, nc),
        in_specs=[pl.BlockSpec((1, ln, PC_W), lambda b, i: (b, chunk(i), 0)),
                  pl.BlockSpec((LANES, LANES), lambda b, i: (0, 0)),
                  pl.BlockSpec((1, LANES), lambda b, i: (0, 0))],
        out_specs=pl.BlockSpec((1, ln, N_HEADS * C_DV), lambda b, i: (b, chunk(i), 0)),
        out_shape=jax.ShapeDtypeStruct((bsz, r, N_HEADS * C_DV), F32),
        scratch_shapes=[pltpu.VMEM((N_HEADS * C_DK, N_HEADS * C_DV), F32)],
        compiler_params=_cparams(("arbitrary", "arbitrary")),
        name="gla_bwd" if rev else "gla_fwd",
    )(pc, wg_pad, bg)


def _out_kernel(a_ref, bf_ref, bb_ref, bo_ref, cf_ref, cb_ref, d_ref, pg_ref, x_ref, mod_ref,
                gn_ref, w_ref, o_ref):
    gmat = _group_matrix(GROUP_W, 64, GROUP_W, 64, 1.0 / 64)
    gn = gn_ref[...]

    def rms64(hsum, gain):
        ms = _dot(hsum * hsum, gmat, HI)
        return hsum * lax.rsqrt(ms + EPS) * gain

    b_out = rms64(bf_ref[0] + bb_ref[0], gn[0:1]) * _sigmoid(bo_ref[0])
    c_out = rms64(cf_ref[0] + cb_ref[0], gn[1:2])
    y = jnp.concatenate([a_ref[0], b_out, c_out, d_ref[0]], axis=-1)
    gate = pg_ref[0]
    y = y * (gate * _sigmoid(gate))
    upd = _dot(y.astype(BF16), w_ref[...])
    o_ref[0] = x_ref[0] + mod_ref[0, 0][2:3] * upd


def _out_projection(a_o, bf, bb, pb, cf, cb, d_o, pg, xc, msel, gains, w_out_b, with_ctx):
    bsz, r, d = xc.shape
    tm = ROW_TILE
    nct = CTX // tm
    first = 0 if with_ctx else nct
    row = lambda w: pl.BlockSpec((1, tm, w), lambda b, i: (b, i + first, 0))
    out_rows = r - first * tm
    return pl.pallas_call(
        _out_kernel,
        grid=(bsz, r // tm - first),
        in_specs=[row(256), row(256), row(256),
                  pl.BlockSpec((1, tm, 256), lambda b, i: (b, i + first, 3)),
                  row(256), row(256), row(256), row(PG_W), row(d),
                  pl.BlockSpec((1, 1, 3, d), lambda b, i: (b, jnp.where(i + first >= nct, 1, 0), 0, 0)),
                  pl.BlockSpec((8, 256), lambda b, i: (0, 0)),
                  pl.BlockSpec((GROUP_W * 4, d), lambda b, i: (0, 0))],
        out_specs=pl.BlockSpec((1, tm, d), lambda b, i: (b, i, 0)),
        out_shape=jax.ShapeDtypeStruct((bsz, out_rows, d), F32),
        compiler_params=_cparams(("arbitrary", "arbitrary")),
        name="out_projection",
    )(a_o, bf, bb, pb, cf, cb, d_o, pg, xc, msel, gains, w_out_b)


def _pad_cols(w, width):
    return jnp.pad(w, ((0, 0), (0, width - w.shape[1])))


def _layout_w_in(w):
    a = w[:, 0:768]
    bq, bk, bv = w[:, 768:1024], w[:, 1024:1280], w[:, 1280:1536]
    bg, bo = w[:, 1536:1552], w[:, 1552:1808]
    cq, ck, cv, clr = w[:, 1808:1936], w[:, 1936:2064], w[:, 2064:2320], w[:, 2320:2352]
    dd = w[:, 2352:2864]
    gg = w[:, 2864:3888]
    return jnp.concatenate([a, bq, bk, bv, bo, _pad_cols(bg, LANES), cq, ck, cv, _pad_cols(clr, LANES), dd, gg],
                           axis=1)


def _rope_tables(t, hd):
    nq = hd // 4
    inv = ROPE_THETA ** (-jnp.arange(nq, dtype=F32) / nq)
    pos = jnp.arange(t, dtype=jnp.int32)
    rows = (pos // GRID_W).astype(F32)
    cols = (pos % GRID_W).astype(F32)
    lane = jnp.arange(256)
    d = lane % hd
    use_col = (d // (hd // 2)) == 1
    within = d % (hd // 2)
    fidx = within % nq
    first = within < nq
    ang = jnp.where(use_col[None, :], cols[:, None], rows[:, None]) * inv[fidx][None, :]
    cos = jnp.cos(ang)
    sin = jnp.where(first[None, :], -jnp.sin(ang), jnp.sin(ang))
    cos = jnp.concatenate([jnp.ones((CTX, 256), F32), cos], axis=0)
    sin = jnp.concatenate([jnp.zeros((CTX, 256), F32), sin], axis=0)
    return cos, sin


def _tile_lanes(g, width):
    return jnp.tile(g, width // g.shape[0])


def kernel(x, c, ctx, c_ctx, w_mod, b_mod, norm_g, w_in, w_out, a_qn, a_kn, a_lam, a_subln, b_conv_w, b_conv_b, b_gate_b, b_outn, c_wg, c_bg, c_outn, d_qn, d_kn, d_sink):
    bsz, t, d = x.shape
    depth = w_mod.shape[0]
    assert d == D_MODEL and ctx.shape[1] == CTX and t % ROW_TILE == 0 and bsz + 1 <= 8

    s_in = jnp.zeros((8, d), F32).at[:bsz].set(c).at[bsz].set(c_ctx)
    mod = _modulation(s_in, w_mod, b_mod).reshape(depth, 8, 3, d)
    cos_a, sin_a = _rope_tables(t, A_DK)
    cos_d, sin_d = _rope_tables(t, D_HD)
    tabs = (cos_a, sin_a, cos_d, sin_d)

    xc = jnp.concatenate([ctx, x], axis=1)
    for l in range(depth):
        with_ctx = l < depth - 1
        lam_init = 0.8 - 0.6 * math.exp(-0.3 * l)
        msel = jnp.stack([jnp.broadcast_to(mod[l, bsz], (bsz, 3, d)), mod[l, :bsz]], axis=1)
        w_in_p = _layout_w_in(w_in[l]).astype(BF16)
        pa, pb, pc, pd, pg = _in_projection(xc, msel, norm_g[l], w_in_p)

        gains_att = jnp.zeros((8, 256), F32)
        gains_att = gains_att.at[0].set(_tile_lanes(a_qn[l], 256)).at[1].set(_tile_lanes(a_kn[l], 256))
        gains_att = gains_att.at[2].set(_tile_lanes(d_qn[l], 256)).at[3].set(_tile_lanes(d_kn[l], 256))
        qa, kta, va, qd, kd, vd = _prep_attn(pa, pd, tabs, gains_att)
        a_o = _diff_attention(qa, kta, va, a_lam[l], a_subln[l], lam_init, with_ctx)
        sink = jnp.zeros((1, LANES), F32).at[0, :N_HEADS].set(d_sink[l])
        d_o = _window_attention(qd, kd, vd, sink, with_ctx)

        gate_b = jnp.zeros((1, LANES), F32).at[0, :16].set(b_gate_b[l].reshape(16))
        conv_b = b_conv_b[l].reshape(1, -1)
        bf = _mlstm(pb, b_conv_w[l], conv_b, gate_b, rev=False)
        bb = _mlstm(pb, b_conv_w[l], conv_b, gate_b, rev=True)

        c_fb = []
        for z in range(2):
            wg_pad = jnp.zeros((LANES, LANES), F32).at[z * GLA_RANK:(z + 1) * GLA_RANK].set(c_wg[l, z])
            c_fb.append(_gla(pc, wg_pad, c_bg[l, z].reshape(1, LANES), rev=bool(z)))

        gains_out = jnp.zeros((8, 256), F32)
        gains_out = gains_out.at[0].set(_tile_lanes(b_outn[l], 256)).at[1].set(_tile_lanes(c_outn[l], 256))
        xc = _out_projection(a_o, bf, bb, pb, c_fb[0], c_fb[1], d_o, pg, xc, msel, gains_out,
                             w_out[l].astype(BF16), with_ctx)
    return xc
```

```python
import math
from functools import partial

import jax
import jax.numpy as jnp
from jax import lax
from jax.experimental import pallas as pl
from jax.experimental.pallas import tpu as pltpu

F32 = jnp.float32
BF16 = jnp.bfloat16
HI = lax.Precision.HIGHEST

D_MODEL = 1024
CTX = 256
GRID_W = 64
GROUP_W = 256
N_HEADS = 4
A_DK = 32
A_DV = 64
B_DH = 64
C_DK = 32
C_DV = 64
GLA_RANK = 16
GLA_TAU = 16.0
D_HD = 64
WINDOW = 128
ROPE_THETA = 10000.0
EPS = 1e-6

LANES = 128
VMEM_LIMIT = 56 * 1024 * 1024

PA_W = 768
PB_W = 1152
PC_W = 640
PD_W = 512
PG_W = 1024
P_W = PA_W + PB_W + PC_W + PD_W + PG_W

ROW_TILE = 256
SCAN_CHUNK = 128
GLA_SUB = 16
ATT_TQ = 256
ATT_TK = 512
ATT_TILES_PER_STEP = 1

NEG_BIG = -1e30
LOG2E = 1.4426950408889634
A_VT = 80
N_CHAINS = 2 * N_HEADS
ATT_SKEW = 8


def _cparams(sem):
    return pltpu.CompilerParams(dimension_semantics=sem, vmem_limit_bytes=VMEM_LIMIT)


def _log_sigmoid(x):
    return jnp.minimum(x, 0.0) - jnp.log(1.0 + jnp.exp(-jnp.abs(x)))


def _sigmoid(x):
    return 1.0 / (1.0 + jnp.exp(-x))


def _lane_iota(shape):
    return lax.broadcasted_iota(jnp.int32, shape, len(shape) - 1)


def _head_mask(width, per_head, h, rows=1):
    lane = _lane_iota((rows, width))
    return (lane >= h * per_head) & (lane < (h + 1) * per_head)


def _group_matrix(n_rows, rows_per_group, n_cols, cols_per_group, value=1.0):
    r = lax.broadcasted_iota(jnp.int32, (n_rows, n_cols), 0)
    c = lax.broadcasted_iota(jnp.int32, (n_rows, n_cols), 1)
    same = jnp.zeros((n_rows, n_cols), jnp.bool_)
    for g in range(n_rows // rows_per_group):
        same = same | ((r >= g * rows_per_group) & (r < (g + 1) * rows_per_group)
                       & (c >= g * cols_per_group) & (c < (g + 1) * cols_per_group))
    return jnp.where(same, value, 0.0).astype(F32)


def _dot(a, b, precision=None):
    return jnp.dot(a, b, preferred_element_type=F32, precision=precision)


def _dot_nt(a, b, precision=None):
    return lax.dot_general(a, b, (((1,), (1,)), ((), ())), preferred_element_type=F32,
                           precision=precision)


def _mod_kernel(s_ref, w_ref, b_ref, o_ref):
    s = s_ref[...]
    s = s * _sigmoid(s)
    o_ref[0] = _dot(s, w_ref[0], HI) + b_ref[0]


def _modulation(s_in, w_mod, b_mod):
    depth, d, n = w_mod.shape
    tn = 1024
    return pl.pallas_call(
        _mod_kernel,
        grid=(depth, n // tn),
        in_specs=[pl.BlockSpec((8, d), lambda l, j: (0, 0)),
                  pl.BlockSpec((1, d, tn), lambda l, j: (l, 0, j)),
                  pl.BlockSpec((1, 1, tn), lambda l, j: (l, 0, j))],
        out_specs=pl.BlockSpec((1, 8, tn), lambda l, j: (l, 0, j)),
        out_shape=jax.ShapeDtypeStruct((depth, 8, n), F32),
        compiler_params=_cparams(("arbitrary", "arbitrary")),
        name="modulation",
    )(s_in, w_mod, b_mod.reshape(depth, 1, n))


def _proj_kernel(x_ref, mod_ref, g_ref, w_ref, pa_ref, pb_ref, pc_ref, pd_ref, pg_ref):
    x = x_ref[0]
    ms = jnp.mean(x * x, axis=-1, keepdims=True)
    y = x * lax.rsqrt(ms + EPS) * g_ref[...]
    mod = mod_ref[0, 0]
    hx = y * (1.0 + mod[1:2]) + mod[0:1]
    p = _dot(hx.astype(BF16), w_ref[...])
    o = 0
    for ref, w in ((pa_ref, PA_W), (pb_ref, PB_W), (pc_ref, PC_W), (pd_ref, PD_W), (pg_ref, PG_W)):
        ref[0] = p[:, o:o + w]
        o += w


def _in_projection(xc, msel, norm_g, w_in_p):
    bsz, r, d = xc.shape
    tm = ROW_TILE
    nct = CTX // tm
    row = lambda w: pl.BlockSpec((1, tm, w), lambda b, i: (b, i, 0))
    return pl.pallas_call(
        _proj_kernel,
        grid=(bsz, r // tm),
        in_specs=[row(d),
                  pl.BlockSpec((1, 1, 3, d), lambda b, i: (b, jnp.where(i >= nct, 1, 0), 0, 0)),
                  pl.BlockSpec((1, d), lambda b, i: (0, 0)),
                  pl.BlockSpec((d, P_W), lambda b, i: (0, 0))],
        out_specs=[row(PA_W), row(PB_W), row(PC_W), row(PD_W), row(PG_W)],
        out_shape=[jax.ShapeDtypeStruct((bsz, r, w), F32) for w in (PA_W, PB_W, PC_W, PD_W, PG_W)],
        compiler_params=_cparams(("arbitrary", "arbitrary")),
        name="in_projection",
    )(xc, msel, norm_g.reshape(1, d), w_in_p)


def _norm_rope(x, gain, cos, sin, group, half_pair, scale):
    w = x.shape[-1]
    gmat = _group_matrix(w, group, w, group, 1.0 / group)
    ms = _dot(x * x, gmat, HI)
    xn = x * lax.rsqrt(ms + EPS) * gain
    lane = _lane_iota(xn.shape)
    first = (lane & (2 * half_pair - 1)) < half_pair
    partner = jnp.where(first, pltpu.roll(xn, w - half_pair, 1), pltpu.roll(xn, half_pair, 1))
    out = xn * cos + partner * sin
    return out * scale if scale != 1.0 else out


def _prep_attn_kernel(pa_ref, pd_ref, cosa_ref, sina_ref, cosd_ref, sind_ref, gains_ref,
                      qta_ref, ka_ref, vta_ref, qd_ref, kd_ref, vd_ref):
    pa = pa_ref[0]
    cosa, sina = cosa_ref[...], sina_ref[...]
    gains = gains_ref[...]
    qa = _norm_rope(pa[:, 0:256], gains[0:1], cosa, sina, A_DK, A_DK // 4, A_DK ** -0.5 * LOG2E)
    ka = _norm_rope(pa[:, 256:512], gains[1:2], cosa, sina, A_DK, A_DK // 4, 1.0)
    qta_ref[0] = qa.T.astype(BF16)
    ka_ref[0] = ka.astype(BF16)
    vat = pa[:, 512:768].T
    rows = vat.shape[1]
    ones_rows = jnp.where(lax.broadcasted_iota(jnp.int32, (A_VT - A_DV, rows), 0) == 0, 1.0, 0.0).astype(BF16)
    for h in range(N_HEADS):
        vta_ref[0, h, 0:A_DV, :] = vat[h * A_DV:(h + 1) * A_DV].astype(BF16)
        vta_ref[0, h, A_DV:A_VT, :] = ones_rows
    pd = pd_ref[0]
    cosd, sind = cosd_ref[...], sind_ref[...]
    qd = _norm_rope(pd[:, 0:256], gains[2:3], cosd, sind, D_HD, D_HD // 4, D_HD ** -0.5)
    kd = _norm_rope(pd[:, 256:384], gains[3:4, 0:128], cosd[:, 0:128], sind[:, 0:128], D_HD, D_HD // 4, 1.0)
    qd_ref[0] = qd.astype(BF16)
    kd_ref[0] = kd.astype(BF16)
    vd_ref[0] = pd[:, 384:512].astype(BF16)


def _prep_attn(pa, pd, tabs, gains):
    bsz, r, _ = pa.shape
    tr = ROW_TILE
    row = lambda w: pl.BlockSpec((1, tr, w), lambda b, i: (b, i, 0))
    tab = pl.BlockSpec((tr, 256), lambda b, i: (i, 0))
    return pl.pallas_call(
        _prep_attn_kernel,
        grid=(bsz, r // tr),
        in_specs=[row(PA_W), row(PD_W), tab, tab, tab, tab, pl.BlockSpec((8, 256), lambda b, i: (0, 0))],
        out_specs=[pl.BlockSpec((1, 256, tr), lambda b, i: (b, 0, i)),
                   row(256),
                   pl.BlockSpec((1, N_HEADS, A_VT, tr), lambda b, i: (b, 0, 0, i)),
                   row(256), row(128), row(128)],
        out_shape=[jax.ShapeDtypeStruct((bsz, 256, r), BF16),
                   jax.ShapeDtypeStruct((bsz, r, 256), BF16),
                   jax.ShapeDtypeStruct((bsz, N_HEADS, A_VT, r), BF16),
                   jax.ShapeDtypeStruct((bsz, r, 256), BF16),
                   jax.ShapeDtypeStruct((bsz, r, 128), BF16),
                   jax.ShapeDtypeStruct((bsz, r, 128), BF16)],
        compiler_params=_cparams(("arbitrary", "arbitrary")),
        name="prep_attn",
    )(pa, pd, *tabs, gains)


def _diff_attn_kernel(qt_ref, k_ref, vt_ref, lam_ref, g_ref, o_ref, *scr, first_block, lam_init):
    wq_scr, acc_scr, m_scr = scr[0:N_CHAINS], scr[N_CHAINS:2 * N_CHAINS], scr[2 * N_CHAINS:3 * N_CHAINS]
    i = pl.program_id(1) + first_block
    tq = qt_ref.shape[2]
    step_keys = ATT_TK * ATT_TILES_PER_STEP
    n_steps = jnp.where(i == 0, 0, (k_ref.shape[1] - CTX) // step_keys)
    row = lax.broadcasted_iota(jnp.int32, (LANES, tq), 0)
    for ch in range(N_CHAINS):
        grp, r_in = ch // 4, (ch % 4) * A_DK
        qt = qt_ref[0, grp * LANES:(grp + 1) * LANES, :]
        wq_scr[ch][...] = jnp.where((row >= r_in) & (row < r_in + A_DK), qt, jnp.zeros_like(qt))
        m_scr[ch][...] = jnp.full((8, tq), NEG_BIG, F32)
        acc_scr[ch][...] = jnp.zeros((A_VT, tq), F32)

    def process(tiles):
        units = [(ch, off, size) for off, size in tiles for ch in range(N_CHAINS)]

        def scores(u):
            ch, off, size = units[u]
            grp = ch // 4
            kt = k_ref[0, pl.ds(off, size), grp * LANES:(grp + 1) * LANES]
            return _dot(kt, wq_scr[ch][...])

        s = {u: scores(u) for u in range(min(ATT_SKEW, len(units)))}
        for u, (ch, off, size) in enumerate(units):
            m_old = m_scr[ch][0:1, :]
            m_new = jnp.maximum(m_old, jnp.max(s[u], axis=0, keepdims=True))
            alpha = jnp.exp2(m_old - m_new)
            p = jnp.exp2(s.pop(u) - m_new).astype(BF16)
            if u + ATT_SKEW < len(units):
                s[u + ATT_SKEW] = scores(u + ATT_SKEW)
            pv = _dot(vt_ref[0, ch // 2, :, pl.ds(off, size)], p)
            acc_scr[ch][...] = alpha * acc_scr[ch][...] + pv
            m_scr[ch][...] = jnp.broadcast_to(m_new, (8, tq))

    process([(0, CTX)])

    def step(j, carry):
        process([(pl.multiple_of(CTX + j * step_keys + t * ATT_TK, math.gcd(CTX, ATT_TK)), ATT_TK)
                 for t in range(ATT_TILES_PER_STEP)])
        return carry

    lax.fori_loop(0, n_steps, step, 0)

    lp = lam_ref[...]
    lam = (jnp.exp(jnp.sum(lp[0:1] * lp[1:2], axis=-1, keepdims=True))
           - jnp.exp(jnp.sum(lp[2:3] * lp[3:4], axis=-1, keepdims=True)) + lam_init)
    outs = []
    for h in range(N_HEADS):
        comp = []
        for c in range(2):
            acc = acc_scr[2 * h + c][...]
            comp.append(acc[0:A_DV] / acc[A_DV:A_DV + 1])
        o = comp[0] - lam * comp[1]
        ms = jnp.mean(o * o, axis=0, keepdims=True)
        outs.append(o * lax.rsqrt(ms + EPS) * g_ref[...] * (1.0 - lam_init))
    o_ref[0] = jnp.concatenate(outs, axis=0).T


def _diff_attention(qta, ka, vta, a_lam, a_subln, lam_init, with_ctx):
    bsz, r, _ = ka.shape
    tq = ATT_TQ
    first = 0 if with_ctx else CTX // tq
    nq = r // tq - first
    return pl.pallas_call(
        partial(_diff_attn_kernel, first_block=first, lam_init=lam_init),
        grid=(bsz, nq),
        in_specs=[pl.BlockSpec((1, 256, tq), lambda b, i: (b, 0, i + first)),
                  pl.BlockSpec((1, r, 256), lambda b, i: (b, 0, 0)),
                  pl.BlockSpec((1, N_HEADS, A_VT, r), lambda b, i: (b, 0, 0, 0)),
                  pl.BlockSpec((4, A_DK), lambda b, i: (0, 0)),
                  pl.BlockSpec((A_DV, 1), lambda b, i: (0, 0))],
        out_specs=pl.BlockSpec((1, tq, 256), lambda b, i: (b, i + first, 0)),
        out_shape=jax.ShapeDtypeStruct((bsz, r, 256), F32),
        scratch_shapes=([pltpu.VMEM((LANES, tq), BF16)] * N_CHAINS
                        + [pltpu.VMEM((A_VT, tq), F32)] * N_CHAINS
                        + [pltpu.VMEM((8, tq), F32)] * N_CHAINS),
        compiler_params=_cparams(("arbitrary", "arbitrary")),
        name="diff_attention",
    )(qta, ka, vta, a_lam, a_subln.reshape(A_DV, 1))


def _window_attn_kernel(q_ref, k_ref, v_ref, sink_ref, o_ref, *, first_block):
    i = pl.program_id(1) + first_block
    tq = q_ref.shape[1]
    r = k_ref.shape[1]
    span = tq + 2 * WINDOW
    start = jnp.clip(i * tq - WINDOW, CTX, r - span)
    start = pl.multiple_of(start, WINDOW)
    kwin = k_ref[0, pl.ds(start, span), :]
    vwin = v_ref[0, pl.ds(start, span), :]
    kctx = k_ref[0, 0:CTX, :]
    vctx = v_ref[0, 0:CTX, :]
    qpos = i * tq + lax.broadcasted_iota(jnp.int32, (tq, span), 0)
    kpos = start + lax.broadcasted_iota(jnp.int32, (tq, span), 1)
    valid = (jnp.abs(kpos - qpos) <= WINDOW) & (qpos >= CTX)
    q_all = q_ref[0]
    sink = sink_ref[...]
    outs = []
    for h in range(N_HEADS):
        kv = h // 2
        q = q_all[:, h * D_HD:(h + 1) * D_HD]
        s_loc = _dot_nt(q, kwin[:, kv * D_HD:(kv + 1) * D_HD])
        s_loc = jnp.where(valid, s_loc, -jnp.inf)
        s_ctx = _dot_nt(q, kctx[:, kv * D_HD:(kv + 1) * D_HD])
        sk = sink[0:1, h:h + 1]
        m = jnp.maximum(jnp.maximum(jnp.max(s_loc, axis=-1, keepdims=True),
                                    jnp.max(s_ctx, axis=-1, keepdims=True)), sk)
        p_loc = jnp.exp(s_loc - m)
        p_ctx = jnp.exp(s_ctx - m)
        den = (jnp.sum(p_loc, axis=-1, keepdims=True) + jnp.sum(p_ctx, axis=-1, keepdims=True)
               + jnp.exp(sk - m))
        o = (_dot(p_loc.astype(BF16), vwin[:, kv * D_HD:(kv + 1) * D_HD])
             + _dot(p_ctx.astype(BF16), vctx[:, kv * D_HD:(kv + 1) * D_HD]))
        outs.append(o / den)
    o_ref[0] = jnp.concatenate(outs, axis=-1)


def _window_attention(qd, kd, vd, sink, with_ctx):
    bsz, r, _ = qd.shape
    tq = ROW_TILE
    first = 0 if with_ctx else CTX // tq
    return pl.pallas_call(
        partial(_window_attn_kernel, first_block=first),
        grid=(bsz, r // tq - first),
        in_specs=[pl.BlockSpec((1, tq, 256), lambda b, i: (b, i + first, 0)),
                  pl.BlockSpec((1, r, 128), lambda b, i: (b, 0, 0)),
                  pl.BlockSpec((1, r, 128), lambda b, i: (b, 0, 0)),
                  pl.BlockSpec((1, LANES), lambda b, i: (0, 0))],
        out_specs=pl.BlockSpec((1, tq, 256), lambda b, i: (b, i + first, 0)),
        out_shape=jax.ShapeDtypeStruct((bsz, r, 256), F32),
        compiler_params=_cparams(("arbitrary", "arbitrary")),
        name="window_attention",
    )(qd, kd, vd, sink)


def _chunk_of_step(i, n_chunks, n_ctx_chunks, rev):
    if not rev:
        return i
    return jnp.where(i < n_ctx_chunks, n_ctx_chunks - 1 - i, n_chunks - 1 - (i - n_ctx_chunks))


def _tri(n, rev):
    t = lax.broadcasted_iota(jnp.int32, (n, n), 0)
    s = lax.broadcasted_iota(jnp.int32, (n, n), 1)
    return jnp.where((s >= t) if rev else (s <= t), 1.0, 0.0).astype(F32)


def _mlstm_kernel(p_ref, prev_ref, next_ref, cw_ref, cb_ref, gb_ref, o_ref, c_scr, n_scr, m_scr,
                  *, rev, n_chunks, n_ctx_chunks):
    i = pl.program_id(1)
    chunk = _chunk_of_step(i, n_chunks, n_ctx_chunks, rev)
    ln = p_ref.shape[1]
    hw = N_HEADS * B_DH

    @pl.when(i == 0)
    def _():
        c_scr[...] = jnp.zeros_like(c_scr)
        n_scr[...] = jnp.zeros_like(n_scr)
        m_scr[...] = jnp.zeros_like(m_scr)

    x = p_ref[0, :, 0:2 * hw]
    seg_first = (chunk == 0) | (chunk == n_ctx_chunks)
    seg_last = (chunk == n_ctx_chunks - 1) | (chunk == n_chunks - 1)
    row_prev = jnp.where(seg_first, 0.0, prev_ref[0, 7:8, :])
    row_next = jnp.where(seg_last, 0.0, next_ref[0, 0:1, :])
    ridx = lax.broadcasted_iota(jnp.int32, x.shape, 0)
    x_prev = jnp.where(ridx == 0, row_prev, pltpu.roll(x, 1, 0))
    x_next = jnp.where(ridx == ln - 1, row_next, pltpu.roll(x, ln - 1, 0))
    cw = cw_ref[...]
    y = cw[0:1] * x_prev + cw[1:2] * x + cw[2:3] * x_next + cb_ref[...]
    y = y * _sigmoid(y)
    q = y[:, 0:hw]
    k = y[:, hw:2 * hw] * (B_DH ** -0.5)
    v = p_ref[0, :, 2 * hw:3 * hw]

    g = p_ref[0, :, 4 * hw:4 * hw + LANES] + gb_ref[...]
    gt = g.T
    base = 8 if rev else 0
    tri = _tri(ln, rev)
    lf_c = _log_sigmoid(g)
    lf_r = _log_sigmoid(gt)
    b_c = _dot(tri, lf_c, HI)
    b_r = _dot_nt(lf_r, tri, HI)
    end = 0 if rev else ln - 1
    t_idx = lax.broadcasted_iota(jnp.int32, (ln, ln), 0)
    s_idx = lax.broadcasted_iota(jnp.int32, (ln, ln), 1)
    causal = (s_idx >= t_idx) if rev else (s_idx <= t_idx)

    qb = q.astype(BF16)
    kb = k.astype(BF16)
    vb = v.astype(BF16)
    c_old = c_scr[...]
    n_old = n_scr[0:1, :]
    m_old = m_scr[0:1, :]
    num_inter = _dot(qb, c_old.astype(BF16))
    seg = _group_matrix(hw, B_DH, LANES, 1)
    nq_inter = _dot(q * n_old, seg, HI)

    h_out = jnp.zeros((ln, hw), F32)
    w_end_full = jnp.zeros((ln, hw), F32)
    decay_row = jnp.zeros((1, hw), F32)
    m_new_row = jnp.zeros((1, LANES), F32)
    lane128 = _lane_iota((1, LANES))
    for h in range(N_HEADS):
        hm = _head_mask(hw, B_DH, h)
        bc_h = b_c[:, base + 4 + h:base + 5 + h]
        br_h = b_r[base + 4 + h:base + 5 + h, :]
        li_c = g[:, base + h:base + h + 1]
        li_r = gt[base + h:base + h + 1, :]
        m_h = m_old[:, h:h + 1]
        d = jnp.where(causal, bc_h - br_h + li_r, -jnp.inf)
        m_inter = bc_h + m_h
        m_t = jnp.maximum(m_inter, jnp.max(d, axis=-1, keepdims=True))
        w = jnp.exp(d - m_t)
        g_inter = jnp.exp(m_inter - m_t)
        s = _dot_nt(jnp.where(hm, q, 0.0).astype(BF16), kb) * w
        num = _dot(s.astype(BF16), vb) + g_inter * num_inter
        nq = jnp.sum(s, axis=-1, keepdims=True) + g_inter * nq_inter[:, h:h + 1]
        den = jnp.maximum(jnp.abs(nq), jnp.exp(-m_t))
        h_out = jnp.where(hm, num / den, h_out)
        b_end = bc_h[end:end + 1, :]
        g_c = b_end - bc_h + li_c
        g_r = b_end - br_h + li_r
        m_new = jnp.maximum(b_end + m_h, jnp.max(g_r, axis=-1, keepdims=True))
        w_end_full = jnp.where(hm, jnp.exp(g_c - m_new), w_end_full)
        decay_row = jnp.where(hm, jnp.exp(b_end + m_h - m_new), decay_row)
        m_new_row = jnp.where(lane128 == h, m_new, m_new_row)
    o_ref[0] = h_out

    kw = k * w_end_full
    upd = _dot(kw.T.astype(BF16), vb)
    blk = _group_matrix(hw, B_DH, hw, B_DH)
    decay_col = _dot_nt(blk, decay_row, HI)[:, 0:1] * (1.0 / B_DH)
    c_scr[...] = decay_col * c_old + blk * upd
    n_scr[0:1, :] = decay_row * n_old + jnp.sum(kw, axis=0, keepdims=True)
    m_scr[0:1, :] = m_new_row


def _mlstm(pb, conv_w, conv_b, gate_b, rev):
    bsz, r, _ = pb.shape
    ln = SCAN_CHUNK
    nc, ncc = r // ln, CTX // ln
    hw = N_HEADS * B_DH
    chunk = lambda i: _chunk_of_step(i, nc, ncc, rev)
    per8 = ln // 8
    return pl.pallas_call(
        partial(_mlstm_kernel, rev=rev, n_chunks=nc, n_ctx_chunks=ncc),
        grid=(bsz, nc),
        in_specs=[pl.BlockSpec((1, ln, PB_W), lambda b, i: (b, chunk(i), 0)),
                  pl.BlockSpec((1, 8, 2 * hw), lambda b, i: (b, jnp.maximum(chunk(i) * per8 - 1, 0), 0)),
                  pl.BlockSpec((1, 8, 2 * hw), lambda b, i: (b, jnp.minimum((chunk(i) + 1) * per8, r // 8 - 1), 0)),
                  pl.BlockSpec((3, 2 * hw), lambda b, i: (0, 0)),
                  pl.BlockSpec((1, 2 * hw), lambda b, i: (0, 0)),
                  pl.BlockSpec((1, LANES), lambda b, i: (0, 0))],
        out_specs=pl.BlockSpec((1, ln, hw), lambda b, i: (b, chunk(i), 0)),
        out_shape=jax.ShapeDtypeStruct((bsz, r, hw), F32),
        scratch_shapes=[pltpu.VMEM((hw, hw), F32), pltpu.VMEM((8, hw), F32), pltpu.VMEM((8, LANES), F32)],
        compiler_params=_cparams(("arbitrary", "arbitrary")),
        name="mlstm_bwd" if rev else "mlstm_fwd",
    )(pb, pb, pb, conv_w, conv_b, gate_b)


def _gla_kernel(p_ref, wg_ref, bg_ref, o_ref, s_scr, *, rev, n_chunks, n_ctx_chunks):
    i = pl.program_id(1)
    ln = p_ref.shape[1]
    kw_ = N_HEADS * C_DK
    vw_ = N_HEADS * C_DV
    c = GLA_SUB
    nb = ln // c

    @pl.when(i == 0)
    def _():
        s_scr[...] = jnp.zeros_like(s_scr)

    q = p_ref[0, :, 0:kw_] * (C_DK ** -0.5)
    k = p_ref[0, :, kw_:2 * kw_]
    v = p_ref[0, :, 2 * kw_:2 * kw_ + vw_]
    lr = p_ref[0, :, 2 * kw_ + vw_:2 * kw_ + vw_ + LANES]
    la = _log_sigmoid(_dot(lr, wg_ref[...], HI) + bg_ref[...]) * (1.0 / GLA_TAU)
    bc = _dot(_tri(ln, rev), la, HI)
    vb = v.astype(BF16)
    blk = _group_matrix(kw_, C_DK, vw_, C_DV)
    blk_b = blk.astype(BF16)

    s_old = s_scr[...]
    o_acc = _dot((q * jnp.exp(bc)).astype(BF16), s_old.astype(BF16))

    t_loc = lax.broadcasted_iota(jnp.int32, (c, kw_), 0)
    diag_parts = []
    for b in range(nb):
        sl = slice(b * c, (b + 1) * c)
        bcb, qb_, kb_, vb_ = bc[sl], q[sl], k[sl], v[sl]
        pieces = []
        for t in range(c):
            keep = (t_loc >= t) if rev else (t_loc <= t)
            diff = jnp.minimum(bcb[t:t + 1] - bcb, 0.0)
            pieces.append(jnp.where(keep, qb_[t:t + 1] * kb_ * jnp.exp(diff), 0.0))
        e = jnp.concatenate(pieces, axis=0)
        a = _dot(e.astype(BF16), blk_b)
        a = a.reshape(c, c, vw_) * vb_[None, :, :]
        diag_parts.append(jnp.sum(a, axis=1))
    o_acc = o_acc + jnp.concatenate(diag_parts, axis=0)

    rmask = lax.broadcasted_iota(jnp.int32, (N_HEADS * c, kw_), 0)
    lmask = _lane_iota((N_HEADS * c, kw_))
    rmask_v = lax.broadcasted_iota(jnp.int32, (N_HEADS * c, vw_), 0)
    lmask_v = _lane_iota((N_HEADS * c, vw_))
    same_k = jnp.zeros((N_HEADS * c, kw_), jnp.bool_)
    same_v = jnp.zeros((N_HEADS * c, vw_), jnp.bool_)
    for h in range(N_HEADS):
        same_k = same_k | ((rmask >= h * c) & (rmask < (h + 1) * c) & (lmask >= h * C_DK) & (lmask < (h + 1) * C_DK))
        same_v = same_v | ((rmask_v >= h * c) & (rmask_v < (h + 1) * c) & (lmask_v >= h * C_DV) & (lmask_v < (h + 1) * C_DV))
    off_parts = [jnp.zeros((c, vw_), F32) for _ in range(nb)]
    off_full = jnp.zeros((ln, vw_), F32)
    for j in range(nb):
        if rev:
            if j == 0:
                continue
            rows = slice(0, j * c)
            edge = bc[j * c:j * c + 1]
        else:
            if j == nb - 1:
                continue
            rows = slice((j + 1) * c, ln)
            edge = bc[(j + 1) * c - 1:(j + 1) * c]
        sl = slice(j * c, (j + 1) * c)
        qs = q[rows] * jnp.exp(bc[rows] - edge)
        ks = k[sl] * jnp.exp(edge - bc[sl])
        kbd = jnp.where(same_k, jnp.concatenate([ks] * N_HEADS, axis=0), 0.0)
        vbd = jnp.where(same_v, jnp.concatenate([v[sl]] * N_HEADS, axis=0), 0.0)
        a = _dot_nt(qs.astype(BF16), kbd.astype(BF16))
        contrib = _dot(a.astype(BF16), vbd.astype(BF16))
        n_rows = contrib.shape[0]
        pad = jnp.zeros((ln - n_rows, vw_), F32)
        off_full = off_full + (jnp.concatenate([contrib, pad], axis=0) if rev
                               else jnp.concatenate([pad, contrib], axis=0))
    del off_parts
    o_ref[0] = o_acc + off_full

    end = 0 if rev else ln - 1
    b_end = bc[end:end + 1]
    kd = k * jnp.exp(b_end - bc)
    upd = _dot(kd.T.astype(BF16), vb)
    eye = jnp.where(lax.broadcasted_iota(jnp.int32, (kw_, kw_), 0)
                    == lax.broadcasted_iota(jnp.int32, (kw_, kw_), 1), 1.0, 0.0).astype(F32)
    decay_col = jnp.sum(eye * jnp.exp(b_end), axis=-1, keepdims=True)
    s_scr[...] = decay_col * s_old + blk * upd


def _gla(pc, wg_pad, bg, rev):
    bsz, r, _ = pc.shape
    ln = SCAN_CHUNK
    nc, ncc = r // ln, CTX // ln
    chunk = lambda i: _chunk_of_step(i, nc, ncc, rev)
    return pl.pallas_call(
        partial(_gla_kernel, rev=rev, n_chunks=nc, n_ctx_chunks=ncc),
        grid=(bsz, nc),
        in_specs=[pl.BlockSpec((1, ln, PC_W), lambda b, i: (b, chunk(i), 0)),
                  pl.BlockSpec((LANES, LANES), lambda b, i: (0, 0)),
                  pl.BlockSpec((1, LANES), lambda b, i: (0, 0))],
        out_specs=pl.BlockSpec((1, ln, N_HEADS * C_DV), lambda b, i: (b, chunk(i), 0)),
        out_shape=jax.ShapeDtypeStruct((bsz, r, N_HEADS * C_DV), F32),
        scratch_shapes=[pltpu.VMEM((N_HEADS * C_DK, N_HEADS * C_DV), F32)],
        compiler_params=_cparams(("arbitrary", "arbitrary")),
        name="gla_bwd" if rev else "gla_fwd",
    )(pc, wg_pad, bg)


def _out_kernel(a_ref, bf_ref, bb_ref, bo_ref, cf_ref, cb_ref, d_ref, pg_ref, x_ref, mod_ref,
                gn_ref, w_ref, o_ref):
    gmat = _group_matrix(GROUP_W, 64, GROUP_W, 64, 1.0 / 64)
    gn = gn_ref[...]

    def rms64(hsum, gain):
        ms = _dot(hsum * hsum, gmat, HI)
        return hsum * lax.rsqrt(ms + EPS) * gain

    b_out = rms64(bf_ref[0] + bb_ref[0], gn[0:1]) * _sigmoid(bo_ref[0])
    c_out = rms64(cf_ref[0] + cb_ref[0], gn[1:2])
    y = jnp.concatenate([a_ref[0], b_out, c_out, d_ref[0]], axis=-1)
    gate = pg_ref[0]
    y = y * (gate * _sigmoid(gate))
    upd = _dot(y.astype(BF16), w_ref[...])
    o_ref[0] = x_ref[0] + mod_ref[0, 0][2:3] * upd


def _out_projection(a_o, bf, bb, pb, cf, cb, d_o, pg, xc, msel, gains, w_out_b, with_ctx):
    bsz, r, d = xc.shape
    tm = ROW_TILE
    nct = CTX // tm
    first = 0 if with_ctx else nct
    row = lambda w: pl.BlockSpec((1, tm, w), lambda b, i: (b, i + first, 0))
    out_rows = r - first * tm
    return pl.pallas_call(
        _out_kernel,
        grid=(bsz, r // tm - first),
        in_specs=[row(256), row(256), row(256),
                  pl.BlockSpec((1, tm, 256), lambda b, i: (b, i + first, 3)),
                  row(256), row(256), row(256), row(PG_W), row(d),
                  pl.BlockSpec((1, 1, 3, d), lambda b, i: (b, jnp.where(i + first >= nct, 1, 0), 0, 0)),
                  pl.BlockSpec((8, 256), lambda b, i: (0, 0)),
                  pl.BlockSpec((GROUP_W * 4, d), lambda b, i: (0, 0))],
        out_specs=pl.BlockSpec((1, tm, d), lambda b, i: (b, i, 0)),
        out_shape=jax.ShapeDtypeStruct((bsz, out_rows, d), F32),
        compiler_params=_cparams(("arbitrary", "arbitrary")),
        name="out_projection",
    )(a_o, bf, bb, pb, cf, cb, d_o, pg, xc, msel, gains, w_out_b)


def _pad_cols(w, width):
    return jnp.pad(w, ((0, 0), (0, width - w.shape[1])))


def _layout_w_in(w):
    a = w[:, 0:768]
    bq, bk, bv = w[:, 768:1024], w[:, 1024:1280], w[:, 1280:1536]
    bg, bo = w[:, 1536:1552], w[:, 1552:1808]
    cq, ck, cv, clr = w[:, 1808:1936], w[:, 1936:2064], w[:, 2064:2320], w[:, 2320:2352]
    dd = w[:, 2352:2864]
    gg = w[:, 2864:3888]
    return jnp.concatenate([a, bq, bk, bv, bo, _pad_cols(bg, LANES), cq, ck, cv, _pad_cols(clr, LANES), dd, gg],
                           axis=1)


def _rope_tables(t, hd):
    nq = hd // 4
    inv = ROPE_THETA ** (-jnp.arange(nq, dtype=F32) / nq)
    pos = jnp.arange(t, dtype=jnp.int32)
    rows = (pos // GRID_W).astype(F32)
    cols = (pos % GRID_W).astype(F32)
    lane = jnp.arange(256)
    d = lane % hd
    use_col = (d // (hd // 2)) == 1
    within = d % (hd // 2)
    fidx = within % nq
    first = within < nq
    ang = jnp.where(use_col[None, :], cols[:, None], rows[:, None]) * inv[fidx][None, :]
    cos = jnp.cos(ang)
    sin = jnp.where(first[None, :], -jnp.sin(ang), jnp.sin(ang))
    cos = jnp.concatenate([jnp.ones((CTX, 256), F32), cos], axis=0)
    sin = jnp.concatenate([jnp.zeros((CTX, 256), F32), sin], axis=0)
    return cos, sin


def _tile_lanes(g, width):
    return jnp.tile(g, width // g.shape[0])


def kernel(x, c, ctx, c_ctx, w_mod, b_mod, norm_g, w_in, w_out, a_qn, a_kn, a_lam, a_subln, b_conv_w, b_conv_b, b_gate_b, b_outn, c_wg, c_bg, c_outn, d_qn, d_kn, d_sink):
    bsz, t, d = x.shape
    depth = w_mod.shape[0]
    assert d == D_MODEL and ctx.shape[1] == CTX and t % ROW_TILE == 0 and bsz + 1 <= 8

    s_in = jnp.zeros((8, d), F32).at[:bsz].set(c).at[bsz].set(c_ctx)
    mod = _modulation(s_in, w_mod, b_mod).reshape(depth, 8, 3, d)
    cos_a, sin_a = _rope_tables(t, A_DK)
    cos_d, sin_d = _rope_tables(t, D_HD)
    tabs = (cos_a, sin_a, cos_d, sin_d)

    xc = jnp.concatenate([ctx, x], axis=1)
    for l in range(depth):
        with_ctx = l < depth - 1
        lam_init = 0.8 - 0.6 * math.exp(-0.3 * l)
        msel = jnp.stack([jnp.broadcast_to(mod[l, bsz], (bsz, 3, d)), mod[l, :bsz]], axis=1)
        w_in_p = _layout_w_in(w_in[l]).astype(BF16)
        pa, pb, pc, pd, pg = _in_projection(xc, msel, norm_g[l], w_in_p)

        gains_att = jnp.zeros((8, 256), F32)
        gains_att = gains_att.at[0].set(_tile_lanes(a_qn[l], 256)).at[1].set(_tile_lanes(a_kn[l], 256))
        gains_att = gains_att.at[2].set(_tile_lanes(d_qn[l], 256)).at[3].set(_tile_lanes(d_kn[l], 256))
        qta, ka, vta, qd, kd, vd = _prep_attn(pa, pd, tabs, gains_att)
        a_o = _diff_attention(qta, ka, vta, a_lam[l], a_subln[l], lam_init, with_ctx)
        sink = jnp.zeros((1, LANES), F32).at[0, :N_HEADS].set(d_sink[l])
        d_o = _window_attention(qd, kd, vd, sink, with_ctx)

        gate_b = jnp.zeros((1, LANES), F32).at[0, :16].set(b_gate_b[l].reshape(16))
        conv_b = b_conv_b[l].reshape(1, -1)
        bf = _mlstm(pb, b_conv_w[l], conv_b, gate_b, rev=False)
        bb = _mlstm(pb, b_conv_w[l], conv_b, gate_b, rev=True)

        c_fb = []
        for z in range(2):
            wg_pad = jnp.zeros((LANES, LANES), F32).at[z * GLA_RANK:(z + 1) * GLA_RANK].set(c_wg[l, z])
            c_fb.append(_gla(pc, wg_pad, c_bg[l, z].reshape(1, LANES), rev=bool(z)))

        gains_out = jnp.zeros((8, 256), F32)
        gains_out = gains_out.at[0].set(_tile_lanes(b_outn[l], 256)).at[1].set(_tile_lanes(c_outn[l], 256))
        xc = _out_projection(a_o, bf, bb, pb, c_fb[0], c_fb[1], d_o, pg, xc, msel, gains_out,
                             w_out[l].astype(BF16), with_ctx)
    return xc
```

```python
import math
from functools import partial

import jax
import jax.numpy as jnp
from jax import lax
from jax.experimental import pallas as pl
from jax.experimental.pallas import tpu as pltpu

F32 = jnp.float32
BF16 = jnp.bfloat16
HI = lax.Precision.HIGHEST

D_MODEL = 1024
CTX = 256
GRID_W = 64
GROUP_W = 256
N_HEADS = 4
A_DK = 32
A_DV = 64
B_DH = 64
C_DK = 32
C_DV = 64
GLA_RANK = 16
GLA_TAU = 16.0
D_HD = 64
WINDOW = 128
ROPE_THETA = 10000.0
EPS = 1e-6

LANES = 128
VMEM_LIMIT = 56 * 1024 * 1024

PA_W = 768
PB_W = 1152
PC_W = 640
PD_W = 512
PG_W = 1024
P_W = PA_W + PB_W + PC_W + PD_W + PG_W

ROW_TILE = 256
SCAN_CHUNK = 128
GLA_SUB = 16
ATT_TQ = 256
ATT_TK = 512
ATT_TILES_PER_STEP = 8
ATT_SUM_LIMIT = 2.0 ** 40

NEG_BIG = -1e30
LOG2E = 1.4426950408889634
A_VT = 80
N_CHAINS = 2 * N_HEADS
ATT_SKEW = 8


def _cparams(sem):
    return pltpu.CompilerParams(dimension_semantics=sem, vmem_limit_bytes=VMEM_LIMIT)


def _log_sigmoid(x):
    return jnp.minimum(x, 0.0) - jnp.log(1.0 + jnp.exp(-jnp.abs(x)))


def _sigmoid(x):
    return 1.0 / (1.0 + jnp.exp(-x))


def _lane_iota(shape):
    return lax.broadcasted_iota(jnp.int32, shape, len(shape) - 1)


def _head_mask(width, per_head, h, rows=1):
    lane = _lane_iota((rows, width))
    return (lane >= h * per_head) & (lane < (h + 1) * per_head)


def _group_matrix(n_rows, rows_per_group, n_cols, cols_per_group, value=1.0):
    r = lax.broadcasted_iota(jnp.int32, (n_rows, n_cols), 0)
    c = lax.broadcasted_iota(jnp.int32, (n_rows, n_cols), 1)
    same = jnp.zeros((n_rows, n_cols), jnp.bool_)
    for g in range(n_rows // rows_per_group):
        same = same | ((r >= g * rows_per_group) & (r < (g + 1) * rows_per_group)
                       & (c >= g * cols_per_group) & (c < (g + 1) * cols_per_group))
    return jnp.where(same, value, 0.0).astype(F32)


def _dot(a, b, precision=None):
    return jnp.dot(a, b, preferred_element_type=F32, precision=precision)


def _dot_nt(a, b, precision=None):
    return lax.dot_general(a, b, (((1,), (1,)), ((), ())), preferred_element_type=F32,
                           precision=precision)


def _mod_kernel(s_ref, w_ref, b_ref, o_ref):
    s = s_ref[...]
    s = s * _sigmoid(s)
    o_ref[0] = _dot(s, w_ref[0], HI) + b_ref[0]


def _modulation(s_in, w_mod, b_mod):
    depth, d, n = w_mod.shape
    tn = 1024
    return pl.pallas_call(
        _mod_kernel,
        grid=(depth, n // tn),
        in_specs=[pl.BlockSpec((8, d), lambda l, j: (0, 0)),
                  pl.BlockSpec((1, d, tn), lambda l, j: (l, 0, j)),
                  pl.BlockSpec((1, 1, tn), lambda l, j: (l, 0, j))],
        out_specs=pl.BlockSpec((1, 8, tn), lambda l, j: (l, 0, j)),
        out_shape=jax.ShapeDtypeStruct((depth, 8, n), F32),
        compiler_params=_cparams(("arbitrary", "arbitrary")),
        name="modulation",
    )(s_in, w_mod, b_mod.reshape(depth, 1, n))


def _proj_kernel(x_ref, mod_ref, g_ref, w_ref, pa_ref, pb_ref, pc_ref, pd_ref, pg_ref):
    x = x_ref[0]
    ms = jnp.mean(x * x, axis=-1, keepdims=True)
    y = x * lax.rsqrt(ms + EPS) * g_ref[...]
    mod = mod_ref[0, 0]
    hx = y * (1.0 + mod[1:2]) + mod[0:1]
    p = _dot(hx.astype(BF16), w_ref[...])
    o = 0
    for ref, w in ((pa_ref, PA_W), (pb_ref, PB_W), (pc_ref, PC_W), (pd_ref, PD_W), (pg_ref, PG_W)):
        ref[0] = p[:, o:o + w]
        o += w


def _in_projection(xc, msel, norm_g, w_in_p):
    bsz, r, d = xc.shape
    tm = ROW_TILE
    nct = CTX // tm
    row = lambda w: pl.BlockSpec((1, tm, w), lambda b, i: (b, i, 0))
    return pl.pallas_call(
        _proj_kernel,
        grid=(bsz, r // tm),
        in_specs=[row(d),
                  pl.BlockSpec((1, 1, 3, d), lambda b, i: (b, jnp.where(i >= nct, 1, 0), 0, 0)),
                  pl.BlockSpec((1, d), lambda b, i: (0, 0)),
                  pl.BlockSpec((d, P_W), lambda b, i: (0, 0))],
        out_specs=[row(PA_W), row(PB_W), row(PC_W), row(PD_W), row(PG_W)],
        out_shape=[jax.ShapeDtypeStruct((bsz, r, w), F32) for w in (PA_W, PB_W, PC_W, PD_W, PG_W)],
        compiler_params=_cparams(("arbitrary", "arbitrary")),
        name="in_projection",
    )(xc, msel, norm_g.reshape(1, d), w_in_p)


def _norm_rope(x, gain, cos, sin, group, half_pair, scale):
    w = x.shape[-1]
    gmat = _group_matrix(w, group, w, group, 1.0 / group)
    ms = _dot(x * x, gmat, HI)
    xn = x * lax.rsqrt(ms + EPS) * gain
    lane = _lane_iota(xn.shape)
    first = (lane & (2 * half_pair - 1)) < half_pair
    partner = jnp.where(first, pltpu.roll(xn, w - half_pair, 1), pltpu.roll(xn, half_pair, 1))
    out = xn * cos + partner * sin
    return out * scale if scale != 1.0 else out


def _prep_attn_kernel(pa_ref, pd_ref, cosa_ref, sina_ref, cosd_ref, sind_ref, gains_ref,
                      qta_ref, ka_ref, vta_ref, qd_ref, kd_ref, vd_ref):
    pa = pa_ref[0]
    cosa, sina = cosa_ref[...], sina_ref[...]
    gains = gains_ref[...]
    qa = _norm_rope(pa[:, 0:256], gains[0:1], cosa, sina, A_DK, A_DK // 4, A_DK ** -0.5 * LOG2E)
    ka = _norm_rope(pa[:, 256:512], gains[1:2], cosa, sina, A_DK, A_DK // 4, 1.0)
    qta_ref[0] = qa.T.astype(BF16)
    ka_ref[0] = ka.astype(BF16)
    vat = pa[:, 512:768].T
    rows = vat.shape[1]
    ones_rows = jnp.where(lax.broadcasted_iota(jnp.int32, (A_VT - A_DV, rows), 0) == 0, 1.0, 0.0).astype(BF16)
    for h in range(N_HEADS):
        vta_ref[0, h, 0:A_DV, :] = vat[h * A_DV:(h + 1) * A_DV].astype(BF16)
        vta_ref[0, h, A_DV:A_VT, :] = ones_rows
    pd = pd_ref[0]
    cosd, sind = cosd_ref[...], sind_ref[...]
    qd = _norm_rope(pd[:, 0:256], gains[2:3], cosd, sind, D_HD, D_HD // 4, D_HD ** -0.5)
    kd = _norm_rope(pd[:, 256:384], gains[3:4, 0:128], cosd[:, 0:128], sind[:, 0:128], D_HD, D_HD // 4, 1.0)
    qd_ref[0] = qd.astype(BF16)
    kd_ref[0] = kd.astype(BF16)
    vd_ref[0] = pd[:, 384:512].astype(BF16)


def _prep_attn(pa, pd, tabs, gains):
    bsz, r, _ = pa.shape
    tr = ROW_TILE
    row = lambda w: pl.BlockSpec((1, tr, w), lambda b, i: (b, i, 0))
    tab = pl.BlockSpec((tr, 256), lambda b, i: (i, 0))
    return pl.pallas_call(
        _prep_attn_kernel,
        grid=(bsz, r // tr),
        in_specs=[row(PA_W), row(PD_W), tab, tab, tab, tab, pl.BlockSpec((8, 256), lambda b, i: (0, 0))],
        out_specs=[pl.BlockSpec((1, 256, tr), lambda b, i: (b, 0, i)),
                   row(256),
                   pl.BlockSpec((1, N_HEADS, A_VT, tr), lambda b, i: (b, 0, 0, i)),
                   row(256), row(128), row(128)],
        out_shape=[jax.ShapeDtypeStruct((bsz, 256, r), BF16),
                   jax.ShapeDtypeStruct((bsz, r, 256), BF16),
                   jax.ShapeDtypeStruct((bsz, N_HEADS, A_VT, r), BF16),
                   jax.ShapeDtypeStruct((bsz, r, 256), BF16),
                   jax.ShapeDtypeStruct((bsz, r, 128), BF16),
                   jax.ShapeDtypeStruct((bsz, r, 128), BF16)],
        compiler_params=_cparams(("arbitrary", "arbitrary")),
        name="prep_attn",
    )(pa, pd, *tabs, gains)


def _diff_attn_kernel(qt_ref, k_ref, vt_ref, lam_ref, g_ref, o_ref, *scr, first_block, lam_init):
    wq_scr, acc_scr, m_scr = scr[0:N_CHAINS], scr[N_CHAINS:2 * N_CHAINS], scr[2 * N_CHAINS:3 * N_CHAINS]
    i = pl.program_id(1) + first_block
    tq = qt_ref.shape[2]
    n_tiles = (k_ref.shape[1] - CTX) // ATT_TK
    tiles_per_step = min(ATT_TILES_PER_STEP, n_tiles)
    step_keys = ATT_TK * tiles_per_step
    n_steps = jnp.where(i == 0, 0, n_tiles // tiles_per_step)
    row = lax.broadcasted_iota(jnp.int32, (LANES, tq), 0)
    for ch in range(N_CHAINS):
        grp, r_in = ch // 4, (ch % 4) * A_DK
        qt = qt_ref[0, grp * LANES:(grp + 1) * LANES, :]
        wq_scr[ch][...] = jnp.where((row >= r_in) & (row < r_in + A_DK), qt, jnp.zeros_like(qt))
        m_scr[ch][...] = jnp.full((8, tq), NEG_BIG, F32)
        acc_scr[ch][...] = jnp.zeros((A_VT, tq), F32)

    def process(tiles):
        units = [(ch, off, size) for off, size in tiles for ch in range(N_CHAINS)]

        def scores(u):
            ch, off, size = units[u]
            grp = ch // 4
            kt = k_ref[0, pl.ds(off, size), grp * LANES:(grp + 1) * LANES]
            return _dot(kt, wq_scr[ch][...])

        s = {u: scores(u) for u in range(min(ATT_SKEW, len(units)))}
        for u, (ch, off, size) in enumerate(units):
            m_old = m_scr[ch][0:1, :]
            m_new = jnp.maximum(m_old, jnp.max(s[u], axis=0, keepdims=True))
            alpha = jnp.exp2(m_old - m_new)
            p = jnp.exp2(s.pop(u) - m_new).astype(BF16)
            if u + ATT_SKEW < len(units):
                s[u + ATT_SKEW] = scores(u + ATT_SKEW)
            pv = _dot(vt_ref[0, ch // 2, :, pl.ds(off, size)], p)
            acc_scr[ch][...] = alpha * acc_scr[ch][...] + pv
            m_scr[ch][...] = jnp.broadcast_to(m_new, (8, tq))

    def process_fixed_max(tiles):
        units = [(ch, off, size) for off, size in tiles for ch in range(N_CHAINS)]

        def scores(u):
            ch, off, size = units[u]
            grp = ch // 4
            kt = k_ref[0, pl.ds(off, size), grp * LANES:(grp + 1) * LANES]
            return _dot(kt, wq_scr[ch][...])

        s = {u: scores(u) for u in range(min(ATT_SKEW, len(units)))}
        pv = [None] * N_CHAINS
        for u, (ch, off, size) in enumerate(units):
            p = jnp.exp2(s.pop(u) - m_scr[ch][0:1, :]).astype(BF16)
            if u + ATT_SKEW < len(units):
                s[u + ATT_SKEW] = scores(u + ATT_SKEW)
            d = _dot(vt_ref[0, ch // 2, :, pl.ds(off, size)], p)
            pv[ch] = d if pv[ch] is None else pv[ch] + d
        worst = None
        for ch in range(N_CHAINS):
            row_sum = acc_scr[ch][A_DV:A_DV + 1, :] + pv[ch][A_DV:A_DV + 1, :]
            worst = row_sum if worst is None else jnp.maximum(worst, row_sum)
        ok = jnp.max(worst) < ATT_SUM_LIMIT

        def commit():
            for ch in range(N_CHAINS):
                acc_scr[ch][...] = acc_scr[ch][...] + pv[ch]

        lax.cond(ok, commit, lambda: process_tiles(tiles[0][0], len(tiles)))

    def tile_offset(first, t):
        return pl.multiple_of(first + t * ATT_TK, math.gcd(CTX, ATT_TK))

    def process_tiles(first, n):
        def one(t, carry):
            process([(tile_offset(first, t), ATT_TK)])
            return carry
        lax.fori_loop(0, n, one, 0)

    process([(0, CTX)])

    def step(j, carry):
        process_fixed_max([(tile_offset(CTX + j * step_keys, t), ATT_TK) for t in range(tiles_per_step)])
        return carry

    lax.fori_loop(0, n_steps, step, 0)
    process_tiles(CTX + n_steps * step_keys, jnp.where(i == 0, 0, n_tiles % tiles_per_step))

    lp = lam_ref[...]
    lam = (jnp.exp(jnp.sum(lp[0:1] * lp[1:2], axis=-1, keepdims=True))
           - jnp.exp(jnp.sum(lp[2:3] * lp[3:4], axis=-1, keepdims=True)) + lam_init)
    outs = []
    for h in range(N_HEADS):
        comp = []
        for c in range(2):
            acc = acc_scr[2 * h + c][...]
            comp.append(acc[0:A_DV] / acc[A_DV:A_DV + 1])
        o = comp[0] - lam * comp[1]
        ms = jnp.mean(o * o, axis=0, keepdims=True)
        outs.append(o * lax.rsqrt(ms + EPS) * g_ref[...] * (1.0 - lam_init))
    o_ref[0] = jnp.concatenate(outs, axis=0).T


def _diff_attention(qta, ka, vta, a_lam, a_subln, lam_init, with_ctx):
    bsz, r, _ = ka.shape
    tq = ATT_TQ
    first = 0 if with_ctx else CTX // tq
    nq = r // tq - first
    return pl.pallas_call(
        partial(_diff_attn_kernel, first_block=first, lam_init=lam_init),
        grid=(bsz, nq),
        in_specs=[pl.BlockSpec((1, 256, tq), lambda b, i: (b, 0, i + first)),
                  pl.BlockSpec((1, r, 256), lambda b, i: (b, 0, 0)),
                  pl.BlockSpec((1, N_HEADS, A_VT, r), lambda b, i: (b, 0, 0, 0)),
                  pl.BlockSpec((4, A_DK), lambda b, i: (0, 0)),
                  pl.BlockSpec((A_DV, 1), lambda b, i: (0, 0))],
        out_specs=pl.BlockSpec((1, tq, 256), lambda b, i: (b, i, 0)),
        out_shape=jax.ShapeDtypeStruct((bsz, nq * tq, 256), F32),
        scratch_shapes=([pltpu.VMEM((LANES, tq), BF16)] * N_CHAINS
                        + [pltpu.VMEM((A_VT, tq), F32)] * N_CHAINS
                        + [pltpu.VMEM((8, tq), F32)] * N_CHAINS),
        compiler_params=_cparams(("arbitrary", "arbitrary")),
        name="diff_attention",
    )(qta, ka, vta, a_lam, a_subln.reshape(A_DV, 1))


def _window_attn_kernel(q_ref, k_ref, v_ref, sink_ref, o_ref, *, first_block):
    i = pl.program_id(1) + first_block
    tq = q_ref.shape[1]
    r = k_ref.shape[1]
    span = tq + 2 * WINDOW
    start = jnp.clip(i * tq - WINDOW, CTX, r - span)
    start = pl.multiple_of(start, WINDOW)
    kwin = k_ref[0, pl.ds(start, span), :]
    vwin = v_ref[0, pl.ds(start, span), :]
    kctx = k_ref[0, 0:CTX, :]
    vctx = v_ref[0, 0:CTX, :]
    qpos = i * tq + lax.broadcasted_iota(jnp.int32, (tq, span), 0)
    kpos = start + lax.broadcasted_iota(jnp.int32, (tq, span), 1)
    valid = (jnp.abs(kpos - qpos) <= WINDOW) & (qpos >= CTX)
    q_all = q_ref[0]
    sink = sink_ref[...]
    outs = []
    for h in range(N_HEADS):
        kv = h // 2
        q = q_all[:, h * D_HD:(h + 1) * D_HD]
        s_loc = _dot_nt(q, kwin[:, kv * D_HD:(kv + 1) * D_HD])
        s_loc = jnp.where(valid, s_loc, -jnp.inf)
        s_ctx = _dot_nt(q, kctx[:, kv * D_HD:(kv + 1) * D_HD])
        sk = sink[0:1, h:h + 1]
        m = jnp.maximum(jnp.maximum(jnp.max(s_loc, axis=-1, keepdims=True),
                                    jnp.max(s_ctx, axis=-1, keepdims=True)), sk)
        p_loc = jnp.exp(s_loc - m)
        p_ctx = jnp.exp(s_ctx - m)
        den = (jnp.sum(p_loc, axis=-1, keepdims=True) + jnp.sum(p_ctx, axis=-1, keepdims=True)
               + jnp.exp(sk - m))
        o = (_dot(p_loc.astype(BF16), vwin[:, kv * D_HD:(kv + 1) * D_HD])
             + _dot(p_ctx.astype(BF16), vctx[:, kv * D_HD:(kv + 1) * D_HD]))
        outs.append(o / den)
    o_ref[0] = jnp.concatenate(outs, axis=-1)


def _window_attention(qd, kd, vd, sink, with_ctx):
    bsz, r, _ = qd.shape
    tq = ROW_TILE
    first = 0 if with_ctx else CTX // tq
    return pl.pallas_call(
        partial(_window_attn_kernel, first_block=first),
        grid=(bsz, r // tq - first),
        in_specs=[pl.BlockSpec((1, tq, 256), lambda b, i: (b, i + first, 0)),
                  pl.BlockSpec((1, r, 128), lambda b, i: (b, 0, 0)),
                  pl.BlockSpec((1, r, 128), lambda b, i: (b, 0, 0)),
                  pl.BlockSpec((1, LANES), lambda b, i: (0, 0))],
        out_specs=pl.BlockSpec((1, tq, 256), lambda b, i: (b, i, 0)),
        out_shape=jax.ShapeDtypeStruct((bsz, r - first * tq, 256), F32),
        compiler_params=_cparams(("arbitrary", "arbitrary")),
        name="window_attention",
    )(qd, kd, vd, sink)


def _chunk_of_step(i, n_chunks, n_ctx_chunks, rev):
    if not rev:
        return i
    return jnp.where(i < n_ctx_chunks, n_ctx_chunks - 1 - i, n_chunks - 1 - (i - n_ctx_chunks))


def _tri(n, rev):
    t = lax.broadcasted_iota(jnp.int32, (n, n), 0)
    s = lax.broadcasted_iota(jnp.int32, (n, n), 1)
    return jnp.where((s >= t) if rev else (s <= t), 1.0, 0.0).astype(F32)


def _round_robin(gens):
    live = list(gens)
    while live:
        still = []
        for g in live:
            try:
                next(g)
                still.append(g)
            except StopIteration:
                pass
        live = still


def _scan_call(chain_fn, p, consts, halo_w, out_w, state_shapes, name):
    bsz, r, pw = p.shape
    ln = SCAN_CHUNK
    nc, ncc = r // ln, CTX // ln
    per8 = ln // 8
    dirs = (False, True)

    def kern(*refs):
        n_halo = 4 if halo_w else 0
        p_refs, halo_refs = refs[0:2], refs[2:2 + n_halo]
        const_refs = refs[2 + n_halo:2 + n_halo + len(consts)]
        o_refs = refs[2 + n_halo + len(consts):4 + n_halo + len(consts)]
        scr = refs[4 + n_halo + len(consts):]
        i = pl.program_id(0)

        @pl.when(i == 0)
        def _():
            for s_ref in scr:
                s_ref[...] = jnp.zeros_like(s_ref)

        gens = []
        for b in range(bsz):
            for d, rev in enumerate(dirs):
                c = (b * 2 + d) * len(state_shapes)
                halos = (halo_refs[2 * d], halo_refs[2 * d + 1]) if halo_w else ()
                gens.append(chain_fn(p_refs[d], *halos, *const_refs, o_refs[d], *scr[c:c + len(state_shapes)],
                                     b=b, rev=rev, chunk=_chunk_of_step(i, nc, ncc, rev),
                                     n_chunks=nc, n_ctx_chunks=ncc))
        _round_robin(gens)

    chunk = lambda rev: (lambda i: _chunk_of_step(i, nc, ncc, rev))
    in_specs = [pl.BlockSpec((bsz, ln, pw), lambda i, f=chunk(rev): (0, f(i), 0)) for rev in dirs]
    args = [p, p]
    if halo_w:
        for rev in dirs:
            f = chunk(rev)
            in_specs.append(pl.BlockSpec((bsz, 8, halo_w), lambda i, f=f: (0, jnp.maximum(f(i) * per8 - 1, 0), 0)))
            in_specs.append(pl.BlockSpec((bsz, 8, halo_w),
                                         lambda i, f=f: (0, jnp.minimum((f(i) + 1) * per8, r // 8 - 1), 0)))
            args += [p, p]
    for cst in consts:
        in_specs.append(pl.BlockSpec(cst.shape, lambda i, n=cst.ndim: (0,) * n))
        args.append(cst)
    return pl.pallas_call(
        kern,
        grid=(nc,),
        in_specs=in_specs,
        out_specs=[pl.BlockSpec((bsz, ln, out_w), lambda i, f=chunk(rev): (0, f(i), 0)) for rev in dirs],
        out_shape=[jax.ShapeDtypeStruct((bsz, r, out_w), F32)] * 2,
        scratch_shapes=[pltpu.VMEM(shp, F32) for _ in range(2 * bsz) for shp in state_shapes],
        compiler_params=_cparams(("arbitrary",)),
        name=name,
    )(*args)


def _mlstm_chain(p_ref, prev_ref, next_ref, cw_ref, cb_ref, gb_ref, o_ref, c_scr, n_scr, m_scr,
                 *, b, rev, chunk, n_chunks, n_ctx_chunks):
    ln = p_ref.shape[1]
    hw = N_HEADS * B_DH

    x = p_ref[b, :, 0:2 * hw]
    seg_first = (chunk == 0) | (chunk == n_ctx_chunks)
    seg_last = (chunk == n_ctx_chunks - 1) | (chunk == n_chunks - 1)
    row_prev = jnp.where(seg_first, 0.0, prev_ref[b, 7:8, :])
    row_next = jnp.where(seg_last, 0.0, next_ref[b, 0:1, :])
    ridx = lax.broadcasted_iota(jnp.int32, x.shape, 0)
    x_prev = jnp.where(ridx == 0, row_prev, pltpu.roll(x, 1, 0))
    x_next = jnp.where(ridx == ln - 1, row_next, pltpu.roll(x, ln - 1, 0))
    cw = cw_ref[...]
    y = cw[0:1] * x_prev + cw[1:2] * x + cw[2:3] * x_next + cb_ref[...]
    y = y * _sigmoid(y)
    q = y[:, 0:hw]
    k = y[:, hw:2 * hw] * (B_DH ** -0.5)
    v = p_ref[b, :, 2 * hw:3 * hw]

    g = p_ref[b, :, 4 * hw:4 * hw + LANES] + gb_ref[...]
    gt = g.T
    base = 8 if rev else 0
    tri = _tri(ln, rev)
    lf_c = _log_sigmoid(g)
    lf_r = _log_sigmoid(gt)
    b_c = _dot(tri, lf_c, HI)
    b_r = _dot_nt(lf_r, tri, HI)
    end = 0 if rev else ln - 1
    t_idx = lax.broadcasted_iota(jnp.int32, (ln, ln), 0)
    s_idx = lax.broadcasted_iota(jnp.int32, (ln, ln), 1)
    causal = (s_idx >= t_idx) if rev else (s_idx <= t_idx)

    qb = q.astype(BF16)
    kb = k.astype(BF16)
    vb = v.astype(BF16)
    c_old = c_scr[...]
    n_old = n_scr[0:1, :]
    m_old = m_scr[0:1, :]
    num_inter = _dot(qb, c_old.astype(BF16))
    seg = _group_matrix(hw, B_DH, LANES, 1)
    nq_inter = _dot(q * n_old, seg, HI)
    yield
    qk = [_dot_nt(jnp.where(_head_mask(hw, B_DH, h), q, 0.0).astype(BF16), kb) for h in range(N_HEADS)]
    yield

    h_out = jnp.zeros((ln, hw), F32)
    w_end_full = jnp.zeros((ln, hw), F32)
    decay_row = jnp.zeros((1, hw), F32)
    m_new_row = jnp.zeros((1, LANES), F32)
    lane128 = _lane_iota((1, LANES))
    for h in range(N_HEADS):
        hm = _head_mask(hw, B_DH, h)
        bc_h = b_c[:, base + 4 + h:base + 5 + h]
        br_h = b_r[base + 4 + h:base + 5 + h, :]
        li_c = g[:, base + h:base + h + 1]
        li_r = gt[base + h:base + h + 1, :]
        m_h = m_old[:, h:h + 1]
        d = jnp.where(causal, bc_h - br_h + li_r, -jnp.inf)
        m_inter = bc_h + m_h
        m_t = jnp.maximum(m_inter, jnp.max(d, axis=-1, keepdims=True))
        w = jnp.exp(d - m_t)
        g_inter = jnp.exp(m_inter - m_t)
        s = qk[h] * w
        num = _dot(s.astype(BF16), vb) + g_inter * num_inter
        yield
        nq = jnp.sum(s, axis=-1, keepdims=True) + g_inter * nq_inter[:, h:h + 1]
        den = jnp.maximum(jnp.abs(nq), jnp.exp(-m_t))
        h_out = jnp.where(hm, num / den, h_out)
        b_end = bc_h[end:end + 1, :]
        g_c = b_end - bc_h + li_c
        g_r = b_end - br_h + li_r
        m_new = jnp.maximum(b_end + m_h, jnp.max(g_r, axis=-1, keepdims=True))
        w_end_full = jnp.where(hm, jnp.exp(g_c - m_new), w_end_full)
        decay_row = jnp.where(hm, jnp.exp(b_end + m_h - m_new), decay_row)
        m_new_row = jnp.where(lane128 == h, m_new, m_new_row)
    o_ref[b] = h_out

    kw = k * w_end_full
    upd = _dot(kw.T.astype(BF16), vb)
    blk = _group_matrix(hw, B_DH, hw, B_DH)
    decay_col = jnp.sum(blk * decay_row, axis=-1, keepdims=True) * (1.0 / B_DH)
    c_scr[...] = decay_col * c_old + blk * upd
    n_scr[0:1, :] = decay_row * n_old + jnp.sum(kw, axis=0, keepdims=True)
    m_scr[0:1, :] = m_new_row


def _mlstm(pb, conv_w, conv_b, gate_b):
    hw = N_HEADS * B_DH
    return _scan_call(_mlstm_chain, pb, (conv_w, conv_b, gate_b), 2 * hw, hw,
                      ((hw, hw), (8, hw), (8, LANES)), "mlstm")


def _gla_chain(p_ref, wg_ref, bg_ref, o_ref, s_scr, *, b, rev, chunk, n_chunks, n_ctx_chunks):
    ln = p_ref.shape[1]
    kw_ = N_HEADS * C_DK
    vw_ = N_HEADS * C_DV
    c = GLA_SUB
    nb = ln // c
    z = 1 if rev else 0

    q = p_ref[b, :, 0:kw_] * (C_DK ** -0.5)
    k = p_ref[b, :, kw_:2 * kw_]
    v = p_ref[b, :, 2 * kw_:2 * kw_ + vw_]
    lr = p_ref[b, :, 2 * kw_ + vw_:2 * kw_ + vw_ + LANES]
    la = _log_sigmoid(_dot(lr, wg_ref[z], HI) + bg_ref[z]) * (1.0 / GLA_TAU)
    bc = _dot(_tri(ln, rev), la, HI)
    vb = v.astype(BF16)
    blk = _group_matrix(kw_, C_DK, vw_, C_DV)
    blk_b = blk.astype(BF16)

    s_old = s_scr[...]
    o_acc = _dot((q * jnp.exp(bc)).astype(BF16), s_old.astype(BF16))
    yield

    t_loc = lax.broadcasted_iota(jnp.int32, (c, kw_), 0)
    diag_parts = []
    for sub in range(nb):
        sl = slice(sub * c, (sub + 1) * c)
        bcb, qb_, kb_, vb_ = bc[sl], q[sl], k[sl], v[sl]
        pieces = []
        for t in range(c):
            keep = (t_loc >= t) if rev else (t_loc <= t)
            diff = jnp.minimum(bcb[t:t + 1] - bcb, 0.0)
            pieces.append(jnp.where(keep, qb_[t:t + 1] * kb_ * jnp.exp(diff), 0.0))
        e = jnp.concatenate(pieces, axis=0)
        a = _dot(e.astype(BF16), blk_b)
        a = a.reshape(c, c, vw_) * vb_[None, :, :]
        diag_parts.append(jnp.sum(a, axis=1))
        if sub % 2 == 1:
            yield
    o_acc = o_acc + jnp.concatenate(diag_parts, axis=0)

    rmask = lax.broadcasted_iota(jnp.int32, (N_HEADS * c, kw_), 0)
    lmask = _lane_iota((N_HEADS * c, kw_))
    rmask_v = lax.broadcasted_iota(jnp.int32, (N_HEADS * c, vw_), 0)
    lmask_v = _lane_iota((N_HEADS * c, vw_))
    same_k = jnp.zeros((N_HEADS * c, kw_), jnp.bool_)
    same_v = jnp.zeros((N_HEADS * c, vw_), jnp.bool_)
    for h in range(N_HEADS):
        same_k = same_k | ((rmask >= h * c) & (rmask < (h + 1) * c) & (lmask >= h * C_DK) & (lmask < (h + 1) * C_DK))
        same_v = same_v | ((rmask_v >= h * c) & (rmask_v < (h + 1) * c) & (lmask_v >= h * C_DV) & (lmask_v < (h + 1) * C_DV))
    off_full = jnp.zeros((ln, vw_), F32)
    for j in range(nb):
        if rev:
            if j == 0:
                continue
            rows = slice(0, j * c)
            edge = bc[j * c:j * c + 1]
        else:
            if j == nb - 1:
                continue
            rows = slice((j + 1) * c, ln)
            edge = bc[(j + 1) * c - 1:(j + 1) * c]
        sl = slice(j * c, (j + 1) * c)
        qs = q[rows] * jnp.exp(bc[rows] - edge)
        ks = k[sl] * jnp.exp(edge - bc[sl])
        kbd = jnp.where(same_k, jnp.concatenate([ks] * N_HEADS, axis=0), 0.0)
        vbd = jnp.where(same_v, jnp.concatenate([v[sl]] * N_HEADS, axis=0), 0.0)
        a = _dot_nt(qs.astype(BF16), kbd.astype(BF16))
        contrib = _dot(a.astype(BF16), vbd.astype(BF16))
        n_rows = contrib.shape[0]
        pad = jnp.zeros((ln - n_rows, vw_), F32)
        off_full = off_full + (jnp.concatenate([contrib, pad], axis=0) if rev
                               else jnp.concatenate([pad, contrib], axis=0))
        yield
    o_ref[b] = o_acc + off_full

    end = 0 if rev else ln - 1
    b_end = bc[end:end + 1]
    kd = k * jnp.exp(b_end - bc)
    upd = _dot(kd.T.astype(BF16), vb)
    eye = jnp.where(lax.broadcasted_iota(jnp.int32, (kw_, kw_), 0)
                    == lax.broadcasted_iota(jnp.int32, (kw_, kw_), 1), 1.0, 0.0).astype(F32)
    decay_col = jnp.sum(eye * jnp.exp(b_end), axis=-1, keepdims=True)
    s_scr[...] = decay_col * s_old + blk * upd


def _gla(pc, wg_pad, bg):
    return _scan_call(_gla_chain, pc, (wg_pad, bg), 0, N_HEADS * C_DV,
                      ((N_HEADS * C_DK, N_HEADS * C_DV),), "gla")


def _out_kernel(a_ref, bf_ref, bb_ref, bo_ref, cf_ref, cb_ref, d_ref, pg_ref, x_ref, mod_ref,
                gn_ref, w_ref, o_ref):
    gmat = _group_matrix(GROUP_W, 64, GROUP_W, 64, 1.0 / 64)
    gn = gn_ref[...]

    def rms64(hsum, gain):
        ms = _dot(hsum * hsum, gmat, HI)
        return hsum * lax.rsqrt(ms + EPS) * gain

    b_out = rms64(bf_ref[0] + bb_ref[0], gn[0:1]) * _sigmoid(bo_ref[0])
    c_out = rms64(cf_ref[0] + cb_ref[0], gn[1:2])
    y = jnp.concatenate([a_ref[0], b_out, c_out, d_ref[0]], axis=-1)
    gate = pg_ref[0]
    y = y * (gate * _sigmoid(gate))
    upd = _dot(y.astype(BF16), w_ref[...])
    o_ref[0] = x_ref[0] + mod_ref[0, 0][2:3] * upd


def _out_projection(a_o, bf, bb, pb, cf, cb, d_o, pg, xc, msel, gains, w_out_b, with_ctx):
    bsz, r, d = xc.shape
    tm = ROW_TILE
    nct = CTX // tm
    first = 0 if with_ctx else nct
    row = lambda w: pl.BlockSpec((1, tm, w), lambda b, i: (b, i + first, 0))
    att = pl.BlockSpec((1, tm, 256), lambda b, i: (b, i, 0))
    out_rows = r - first * tm
    return pl.pallas_call(
        _out_kernel,
        grid=(bsz, r // tm - first),
        in_specs=[att, row(256), row(256),
                  pl.BlockSpec((1, tm, 256), lambda b, i: (b, i + first, 3)),
                  row(256), row(256), att, row(PG_W), row(d),
                  pl.BlockSpec((1, 1, 3, d), lambda b, i: (b, jnp.where(i + first >= nct, 1, 0), 0, 0)),
                  pl.BlockSpec((8, 256), lambda b, i: (0, 0)),
                  pl.BlockSpec((GROUP_W * 4, d), lambda b, i: (0, 0))],
        out_specs=pl.BlockSpec((1, tm, d), lambda b, i: (b, i, 0)),
        out_shape=jax.ShapeDtypeStruct((bsz, out_rows, d), F32),
        compiler_params=_cparams(("arbitrary", "arbitrary")),
        name="out_projection",
    )(a_o, bf, bb, pb, cf, cb, d_o, pg, xc, msel, gains, w_out_b)


def _pad_cols(w, width):
    return jnp.pad(w, ((0, 0), (0, width - w.shape[1])))


def _layout_w_in(w):
    a = w[:, 0:768]
    bq, bk, bv = w[:, 768:1024], w[:, 1024:1280], w[:, 1280:1536]
    bg, bo = w[:, 1536:1552], w[:, 1552:1808]
    cq, ck, cv, clr = w[:, 1808:1936], w[:, 1936:2064], w[:, 2064:2320], w[:, 2320:2352]
    dd = w[:, 2352:2864]
    gg = w[:, 2864:3888]
    return jnp.concatenate([a, bq, bk, bv, bo, _pad_cols(bg, LANES), cq, ck, cv, _pad_cols(clr, LANES), dd, gg],
                           axis=1)


def _rope_tables(t, hd):
    nq = hd // 4
    inv = ROPE_THETA ** (-jnp.arange(nq, dtype=F32) / nq)
    pos = jnp.arange(t, dtype=jnp.int32)
    rows = (pos // GRID_W).astype(F32)
    cols = (pos % GRID_W).astype(F32)
    lane = jnp.arange(256)
    d = lane % hd
    use_col = (d // (hd // 2)) == 1
    within = d % (hd // 2)
    fidx = within % nq
    first = within < nq
    ang = jnp.where(use_col[None, :], cols[:, None], rows[:, None]) * inv[fidx][None, :]
    cos = jnp.cos(ang)
    sin = jnp.where(first[None, :], -jnp.sin(ang), jnp.sin(ang))
    cos = jnp.concatenate([jnp.ones((CTX, 256), F32), cos], axis=0)
    sin = jnp.concatenate([jnp.zeros((CTX, 256), F32), sin], axis=0)
    return cos, sin


def _tile_lanes(g, width):
    return jnp.tile(g, width // g.shape[0])


def kernel(x, c, ctx, c_ctx, w_mod, b_mod, norm_g, w_in, w_out, a_qn, a_kn, a_lam, a_subln, b_conv_w, b_conv_b, b_gate_b, b_outn, c_wg, c_bg, c_outn, d_qn, d_kn, d_sink):
    bsz, t, d = x.shape
    depth = w_mod.shape[0]
    assert d == D_MODEL and ctx.shape[1] == CTX and t % ROW_TILE == 0 and bsz + 1 <= 8

    s_in = jnp.zeros((8, d), F32).at[:bsz].set(c).at[bsz].set(c_ctx)
    mod = _modulation(s_in, w_mod, b_mod).reshape(depth, 8, 3, d)
    cos_a, sin_a = _rope_tables(t, A_DK)
    cos_d, sin_d = _rope_tables(t, D_HD)
    tabs = (cos_a, sin_a, cos_d, sin_d)

    xc = jnp.concatenate([ctx, x], axis=1)
    for l in range(depth):
        with_ctx = l < depth - 1
        lam_init = 0.8 - 0.6 * math.exp(-0.3 * l)
        msel = jnp.stack([jnp.broadcast_to(mod[l, bsz], (bsz, 3, d)), mod[l, :bsz]], axis=1)
        w_in_p = _layout_w_in(w_in[l]).astype(BF16)
        pa, pb, pc, pd, pg = _in_projection(xc, msel, norm_g[l], w_in_p)

        gains_att = jnp.zeros((8, 256), F32)
        gains_att = gains_att.at[0].set(_tile_lanes(a_qn[l], 256)).at[1].set(_tile_lanes(a_kn[l], 256))
        gains_att = gains_att.at[2].set(_tile_lanes(d_qn[l], 256)).at[3].set(_tile_lanes(d_kn[l], 256))
        qta, ka, vta, qd, kd, vd = _prep_attn(pa, pd, tabs, gains_att)
        a_o = _diff_attention(qta, ka, vta, a_lam[l], a_subln[l], lam_init, with_ctx)
        sink = jnp.zeros((1, LANES), F32).at[0, :N_HEADS].set(d_sink[l])
        d_o = _window_attention(qd, kd, vd, sink, with_ctx)

        gate_b = jnp.zeros((1, LANES), F32).at[0, :16].set(b_gate_b[l].reshape(16))
        conv_b = b_conv_b[l].reshape(1, -1)
        bf, bb = _mlstm(pb, b_conv_w[l], conv_b, gate_b)

        wg_pad = jnp.zeros((2, LANES, LANES), F32)
        for z in range(2):
            wg_pad = wg_pad.at[z, z * GLA_RANK:(z + 1) * GLA_RANK].set(c_wg[l, z])
        cf, cb = _gla(pc, wg_pad, c_bg[l].reshape(2, 1, LANES))

        gains_out = jnp.zeros((8, 256), F32)
        gains_out = gains_out.at[0].set(_tile_lanes(b_outn[l], 256)).at[1].set(_tile_lanes(c_outn[l], 256))
        xc = _out_projection(a_o, bf, bb, pb, cf, cb, d_o, pg, xc, msel, gains_out,
                             w_out[l].astype(BF16), with_ctx)
    return xc
```

```python
import math
from functools import partial

import jax
import jax.numpy as jnp
from jax import lax
from jax.experimental import pallas as pl
from jax.experimental.pallas import tpu as pltpu

F32 = jnp.float32
BF16 = jnp.bfloat16
HI = lax.Precision.HIGHEST

D_MODEL = 1024
CTX = 256
GRID_W = 64
GROUP_W = 256
N_HEADS = 4
A_DK = 32
A_DV = 64
B_DH = 64
C_DK = 32
C_DV = 64
GLA_RANK = 16
GLA_TAU = 16.0
D_HD = 64
WINDOW = 128
ROPE_THETA = 10000.0
EPS = 1e-6

LANES = 128
VMEM_LIMIT = 56 * 1024 * 1024

PA_W = 768
PB_W = 1152
PC_W = 640
PD_W = 512
PG_W = 1024
P_W = PA_W + PB_W + PC_W + PD_W + PG_W

ROW_TILE = 256
SCAN_CHUNK = 128
GLA_SUB = 16
ATT_TQ = 256
ATT_TK = 512
ATT_TILES_PER_STEP = 8
ATT_SUM_LIMIT = 2.0 ** 40

NEG_BIG = -1e30
LOG2E = 1.4426950408889634
A_VT = 80
N_CHAINS = 2 * N_HEADS
ATT_SKEW = 8


def _cparams(sem):
    return pltpu.CompilerParams(dimension_semantics=sem, vmem_limit_bytes=VMEM_LIMIT)


def _log_sigmoid(x):
    return jnp.minimum(x, 0.0) - jnp.log(1.0 + jnp.exp(-jnp.abs(x)))


def _sigmoid(x):
    return 1.0 / (1.0 + jnp.exp(-x))


def _lane_iota(shape):
    return lax.broadcasted_iota(jnp.int32, shape, len(shape) - 1)


def _head_mask(width, per_head, h, rows=1):
    lane = _lane_iota((rows, width))
    return (lane >= h * per_head) & (lane < (h + 1) * per_head)


def _group_matrix(n_rows, rows_per_group, n_cols, cols_per_group, value=1.0):
    r = lax.broadcasted_iota(jnp.int32, (n_rows, n_cols), 0)
    c = lax.broadcasted_iota(jnp.int32, (n_rows, n_cols), 1)
    same = jnp.zeros((n_rows, n_cols), jnp.bool_)
    for g in range(n_rows // rows_per_group):
        same = same | ((r >= g * rows_per_group) & (r < (g + 1) * rows_per_group)
                       & (c >= g * cols_per_group) & (c < (g + 1) * cols_per_group))
    return jnp.where(same, value, 0.0).astype(F32)


def _dot(a, b, precision=None):
    return jnp.dot(a, b, preferred_element_type=F32, precision=precision)


def _split3(x):
    hi = x.astype(BF16)
    r1 = x - hi.astype(F32)
    mid = r1.astype(BF16)
    lo = (r1 - mid.astype(F32)).astype(BF16)
    return hi, mid, lo


def _dot_f32_by_bf16(a, b_bf16, nt=False):
    f = _dot_nt if nt else _dot
    hi, mid, lo = _split3(a)
    return f(hi, b_bf16) + f(mid, b_bf16) + f(lo, b_bf16)


def _dot_bf16_by_f32(a_bf16, b):
    hi, mid, lo = _split3(b)
    return _dot(a_bf16, hi) + _dot(a_bf16, mid) + _dot(a_bf16, lo)


def _dot_nt(a, b, precision=None):
    return lax.dot_general(a, b, (((1,), (1,)), ((), ())), preferred_element_type=F32,
                           precision=precision)


def _mod_kernel(s_ref, w_ref, b_ref, o_ref):
    s = s_ref[...]
    s = s * _sigmoid(s)
    o_ref[0] = _dot(s, w_ref[0], HI) + b_ref[0]


def _modulation(s_in, w_mod, b_mod):
    depth, d, n = w_mod.shape
    tn = 1024
    return pl.pallas_call(
        _mod_kernel,
        grid=(depth, n // tn),
        in_specs=[pl.BlockSpec((8, d), lambda l, j: (0, 0)),
                  pl.BlockSpec((1, d, tn), lambda l, j: (l, 0, j)),
                  pl.BlockSpec((1, 1, tn), lambda l, j: (l, 0, j))],
        out_specs=pl.BlockSpec((1, 8, tn), lambda l, j: (l, 0, j)),
        out_shape=jax.ShapeDtypeStruct((depth, 8, n), F32),
        compiler_params=_cparams(("arbitrary", "arbitrary")),
        name="modulation",
    )(s_in, w_mod, b_mod.reshape(depth, 1, n))


def _proj_kernel(x_ref, mod_ref, g_ref, w_ref, pa_ref, pb_ref, pc_ref, pd_ref, pg_ref):
    x = x_ref[0]
    ms = jnp.mean(x * x, axis=-1, keepdims=True)
    y = x * lax.rsqrt(ms + EPS) * g_ref[...]
    mod = mod_ref[0, 0]
    hx = y * (1.0 + mod[1:2]) + mod[0:1]
    p = _dot(hx.astype(BF16), w_ref[...])
    o = 0
    for ref, w in ((pa_ref, PA_W), (pb_ref, PB_W), (pc_ref, PC_W), (pd_ref, PD_W), (pg_ref, PG_W)):
        ref[0] = p[:, o:o + w]
        o += w


def _in_projection(xc, msel, norm_g, w_in_p):
    bsz, r, d = xc.shape
    tm = ROW_TILE
    nct = CTX // tm
    row = lambda w: pl.BlockSpec((1, tm, w), lambda b, i: (b, i, 0))
    return pl.pallas_call(
        _proj_kernel,
        grid=(bsz, r // tm),
        in_specs=[row(d),
                  pl.BlockSpec((1, 1, 3, d), lambda b, i: (b, jnp.where(i >= nct, 1, 0), 0, 0)),
                  pl.BlockSpec((1, d), lambda b, i: (0, 0)),
                  pl.BlockSpec((d, P_W), lambda b, i: (0, 0))],
        out_specs=[row(PA_W), row(PB_W), row(PC_W), row(PD_W), row(PG_W)],
        out_shape=[jax.ShapeDtypeStruct((bsz, r, w), F32) for w in (PA_W, PB_W, PC_W, PD_W, PG_W)],
        compiler_params=_cparams(("arbitrary", "arbitrary")),
        name="in_projection",
    )(xc, msel, norm_g.reshape(1, d), w_in_p)


def _norm_rope(x, gain, cos, sin, group, half_pair, scale):
    w = x.shape[-1]
    gmat = _group_matrix(w, group, w, group, 1.0 / group)
    ms = _dot(x * x, gmat, HI)
    xn = x * lax.rsqrt(ms + EPS) * gain
    lane = _lane_iota(xn.shape)
    first = (lane & (2 * half_pair - 1)) < half_pair
    partner = jnp.where(first, pltpu.roll(xn, w - half_pair, 1), pltpu.roll(xn, half_pair, 1))
    out = xn * cos + partner * sin
    return out * scale if scale != 1.0 else out


def _prep_attn_kernel(pa_ref, pd_ref, cosa_ref, sina_ref, cosd_ref, sind_ref, gains_ref,
                      qta_ref, ka_ref, vta_ref, qd_ref, kd_ref, vd_ref):
    pa = pa_ref[0]
    cosa, sina = cosa_ref[...], sina_ref[...]
    gains = gains_ref[...]
    qa = _norm_rope(pa[:, 0:256], gains[0:1], cosa, sina, A_DK, A_DK // 4, A_DK ** -0.5 * LOG2E)
    ka = _norm_rope(pa[:, 256:512], gains[1:2], cosa, sina, A_DK, A_DK // 4, 1.0)
    qta_ref[0] = qa.T.astype(BF16)
    ka_ref[0] = ka.astype(BF16)
    vat = pa[:, 512:768].T
    rows = vat.shape[1]
    ones_rows = jnp.where(lax.broadcasted_iota(jnp.int32, (A_VT - A_DV, rows), 0) == 0, 1.0, 0.0).astype(BF16)
    for h in range(N_HEADS):
        vta_ref[0, h, 0:A_DV, :] = vat[h * A_DV:(h + 1) * A_DV].astype(BF16)
        vta_ref[0, h, A_DV:A_VT, :] = ones_rows
    pd = pd_ref[0]
    cosd, sind = cosd_ref[...], sind_ref[...]
    qd = _norm_rope(pd[:, 0:256], gains[2:3], cosd, sind, D_HD, D_HD // 4, D_HD ** -0.5)
    kd = _norm_rope(pd[:, 256:384], gains[3:4, 0:128], cosd[:, 0:128], sind[:, 0:128], D_HD, D_HD // 4, 1.0)
    qd_ref[0] = qd.astype(BF16)
    kd_ref[0] = kd.astype(BF16)
    vd_ref[0] = pd[:, 384:512].astype(BF16)


def _prep_attn(pa, pd, tabs, gains):
    bsz, r, _ = pa.shape
    tr = ROW_TILE
    row = lambda w: pl.BlockSpec((1, tr, w), lambda b, i: (b, i, 0))
    tab = pl.BlockSpec((tr, 256), lambda b, i: (i, 0))
    return pl.pallas_call(
        _prep_attn_kernel,
        grid=(bsz, r // tr),
        in_specs=[row(PA_W), row(PD_W), tab, tab, tab, tab, pl.BlockSpec((8, 256), lambda b, i: (0, 0))],
        out_specs=[pl.BlockSpec((1, 256, tr), lambda b, i: (b, 0, i)),
                   row(256),
                   pl.BlockSpec((1, N_HEADS, A_VT, tr), lambda b, i: (b, 0, 0, i)),
                   row(256), row(128), row(128)],
        out_shape=[jax.ShapeDtypeStruct((bsz, 256, r), BF16),
                   jax.ShapeDtypeStruct((bsz, r, 256), BF16),
                   jax.ShapeDtypeStruct((bsz, N_HEADS, A_VT, r), BF16),
                   jax.ShapeDtypeStruct((bsz, r, 256), BF16),
                   jax.ShapeDtypeStruct((bsz, r, 128), BF16),
                   jax.ShapeDtypeStruct((bsz, r, 128), BF16)],
        compiler_params=_cparams(("arbitrary", "arbitrary")),
        name="prep_attn",
    )(pa, pd, *tabs, gains)


def _diff_attn_kernel(qt_ref, k_ref, vt_ref, lam_ref, g_ref, o_ref, *scr, first_block, lam_init):
    wq_scr, acc_scr, m_scr = scr[0:N_CHAINS], scr[N_CHAINS:2 * N_CHAINS], scr[2 * N_CHAINS:3 * N_CHAINS]
    i = pl.program_id(1) + first_block
    tq = qt_ref.shape[2]
    n_tiles = (k_ref.shape[1] - CTX) // ATT_TK
    tiles_per_step = min(ATT_TILES_PER_STEP, n_tiles)
    step_keys = ATT_TK * tiles_per_step
    n_steps = jnp.where(i == 0, 0, n_tiles // tiles_per_step)
    row = lax.broadcasted_iota(jnp.int32, (LANES, tq), 0)
    for ch in range(N_CHAINS):
        grp, r_in = ch // 4, (ch % 4) * A_DK
        qt = qt_ref[0, grp * LANES:(grp + 1) * LANES, :]
        wq_scr[ch][...] = jnp.where((row >= r_in) & (row < r_in + A_DK), qt, jnp.zeros_like(qt))
        m_scr[ch][...] = jnp.full((8, tq), NEG_BIG, F32)
        acc_scr[ch][...] = jnp.zeros((A_VT, tq), F32)

    def process(tiles):
        units = [(ch, off, size) for off, size in tiles for ch in range(N_CHAINS)]

        def scores(u):
            ch, off, size = units[u]
            grp = ch // 4
            kt = k_ref[0, pl.ds(off, size), grp * LANES:(grp + 1) * LANES]
            return _dot(kt, wq_scr[ch][...])

        s = {u: scores(u) for u in range(min(ATT_SKEW, len(units)))}
        for u, (ch, off, size) in enumerate(units):
            m_old = m_scr[ch][0:1, :]
            m_new = jnp.maximum(m_old, jnp.max(s[u], axis=0, keepdims=True))
            alpha = jnp.exp2(m_old - m_new)
            p = jnp.exp2(s.pop(u) - m_new).astype(BF16)
            if u + ATT_SKEW < len(units):
                s[u + ATT_SKEW] = scores(u + ATT_SKEW)
            pv = _dot(vt_ref[0, ch // 2, :, pl.ds(off, size)], p)
            acc_scr[ch][...] = alpha * acc_scr[ch][...] + pv
            m_scr[ch][...] = jnp.broadcast_to(m_new, (8, tq))

    def process_fixed_max(tiles):
        units = [(ch, off, size) for off, size in tiles for ch in range(N_CHAINS)]

        def scores(u):
            ch, off, size = units[u]
            grp = ch // 4
            kt = k_ref[0, pl.ds(off, size), grp * LANES:(grp + 1) * LANES]
            return _dot(kt, wq_scr[ch][...])

        s = {u: scores(u) for u in range(min(ATT_SKEW, len(units)))}
        pv = [None] * N_CHAINS
        for u, (ch, off, size) in enumerate(units):
            p = jnp.exp2(s.pop(u) - m_scr[ch][0:1, :]).astype(BF16)
            if u + ATT_SKEW < len(units):
                s[u + ATT_SKEW] = scores(u + ATT_SKEW)
            d = _dot(vt_ref[0, ch // 2, :, pl.ds(off, size)], p)
            pv[ch] = d if pv[ch] is None else pv[ch] + d
        worst = None
        for ch in range(N_CHAINS):
            row_sum = acc_scr[ch][A_DV:A_DV + 1, :] + pv[ch][A_DV:A_DV + 1, :]
            worst = row_sum if worst is None else jnp.maximum(worst, row_sum)
        ok = jnp.max(worst) < ATT_SUM_LIMIT

        def commit():
            for ch in range(N_CHAINS):
                acc_scr[ch][...] = acc_scr[ch][...] + pv[ch]

        lax.cond(ok, commit, lambda: process_tiles(tiles[0][0], len(tiles)))

    def tile_offset(first, t):
        return pl.multiple_of(first + t * ATT_TK, math.gcd(CTX, ATT_TK))

    def process_tiles(first, n):
        def one(t, carry):
            process([(tile_offset(first, t), ATT_TK)])
            return carry
        lax.fori_loop(0, n, one, 0)

    process([(0, CTX)])

    def step(j, carry):
        process_fixed_max([(tile_offset(CTX + j * step_keys, t), ATT_TK) for t in range(tiles_per_step)])
        return carry

    lax.fori_loop(0, n_steps, step, 0)
    process_tiles(CTX + n_steps * step_keys, jnp.where(i == 0, 0, n_tiles % tiles_per_step))

    lp = lam_ref[...]
    lam = (jnp.exp(jnp.sum(lp[0:1] * lp[1:2], axis=-1, keepdims=True))
           - jnp.exp(jnp.sum(lp[2:3] * lp[3:4], axis=-1, keepdims=True)) + lam_init)
    outs = []
    for h in range(N_HEADS):
        comp = []
        for c in range(2):
            acc = acc_scr[2 * h + c][...]
            comp.append(acc[0:A_DV] / acc[A_DV:A_DV + 1])
        o = comp[0] - lam * comp[1]
        ms = jnp.mean(o * o, axis=0, keepdims=True)
        outs.append(o * lax.rsqrt(ms + EPS) * g_ref[...] * (1.0 - lam_init))
    o_ref[0] = jnp.concatenate(outs, axis=0).T


def _diff_attention(qta, ka, vta, a_lam, a_subln, lam_init, with_ctx):
    bsz, r, _ = ka.shape
    tq = ATT_TQ
    first = 0 if with_ctx else CTX // tq
    nq = r // tq - first
    return pl.pallas_call(
        partial(_diff_attn_kernel, first_block=first, lam_init=lam_init),
        grid=(bsz, nq),
        in_specs=[pl.BlockSpec((1, 256, tq), lambda b, i: (b, 0, i + first)),
                  pl.BlockSpec((1, r, 256), lambda b, i: (b, 0, 0)),
                  pl.BlockSpec((1, N_HEADS, A_VT, r), lambda b, i: (b, 0, 0, 0)),
                  pl.BlockSpec((4, A_DK), lambda b, i: (0, 0)),
                  pl.BlockSpec((A_DV, 1), lambda b, i: (0, 0))],
        out_specs=pl.BlockSpec((1, tq, 256), lambda b, i: (b, i, 0)),
        out_shape=jax.ShapeDtypeStruct((bsz, nq * tq, 256), F32),
        scratch_shapes=([pltpu.VMEM((LANES, tq), BF16)] * N_CHAINS
                        + [pltpu.VMEM((A_VT, tq), F32)] * N_CHAINS
                        + [pltpu.VMEM((8, tq), F32)] * N_CHAINS),
        compiler_params=_cparams(("arbitrary", "arbitrary")),
        name="diff_attention",
    )(qta, ka, vta, a_lam, a_subln.reshape(A_DV, 1))


def _window_attn_kernel(q_ref, k_ref, v_ref, sink_ref, o_ref, *, first_block):
    i = pl.program_id(1) + first_block
    tq = q_ref.shape[1]
    r = k_ref.shape[1]
    span = tq + 2 * WINDOW
    start = jnp.clip(i * tq - WINDOW, CTX, r - span)
    start = pl.multiple_of(start, WINDOW)
    kwin = k_ref[0, pl.ds(start, span), :]
    vwin = v_ref[0, pl.ds(start, span), :]
    kctx = k_ref[0, 0:CTX, :]
    vctx = v_ref[0, 0:CTX, :]
    qpos = i * tq + lax.broadcasted_iota(jnp.int32, (tq, span), 0)
    kpos = start + lax.broadcasted_iota(jnp.int32, (tq, span), 1)
    valid = (jnp.abs(kpos - qpos) <= WINDOW) & (qpos >= CTX)
    q_all = q_ref[0]
    sink = sink_ref[...]
    outs = []
    for h in range(N_HEADS):
        kv = h // 2
        q = q_all[:, h * D_HD:(h + 1) * D_HD]
        s_loc = _dot_nt(q, kwin[:, kv * D_HD:(kv + 1) * D_HD])
        s_loc = jnp.where(valid, s_loc, -jnp.inf)
        s_ctx = _dot_nt(q, kctx[:, kv * D_HD:(kv + 1) * D_HD])
        sk = sink[0:1, h:h + 1]
        m = jnp.maximum(jnp.maximum(jnp.max(s_loc, axis=-1, keepdims=True),
                                    jnp.max(s_ctx, axis=-1, keepdims=True)), sk)
        p_loc = jnp.exp(s_loc - m)
        p_ctx = jnp.exp(s_ctx - m)
        den = (jnp.sum(p_loc, axis=-1, keepdims=True) + jnp.sum(p_ctx, axis=-1, keepdims=True)
               + jnp.exp(sk - m))
        o = (_dot(p_loc.astype(BF16), vwin[:, kv * D_HD:(kv + 1) * D_HD])
             + _dot(p_ctx.astype(BF16), vctx[:, kv * D_HD:(kv + 1) * D_HD]))
        outs.append(o / den)
    o_ref[0] = jnp.concatenate(outs, axis=-1)


def _window_attention(qd, kd, vd, sink, with_ctx):
    bsz, r, _ = qd.shape
    tq = ROW_TILE
    first = 0 if with_ctx else CTX // tq
    return pl.pallas_call(
        partial(_window_attn_kernel, first_block=first),
        grid=(bsz, r // tq - first),
        in_specs=[pl.BlockSpec((1, tq, 256), lambda b, i: (b, i + first, 0)),
                  pl.BlockSpec((1, r, 128), lambda b, i: (b, 0, 0)),
                  pl.BlockSpec((1, r, 128), lambda b, i: (b, 0, 0)),
                  pl.BlockSpec((1, LANES), lambda b, i: (0, 0))],
        out_specs=pl.BlockSpec((1, tq, 256), lambda b, i: (b, i, 0)),
        out_shape=jax.ShapeDtypeStruct((bsz, r - first * tq, 256), F32),
        compiler_params=_cparams(("arbitrary", "arbitrary")),
        name="window_attention",
    )(qd, kd, vd, sink)


def _chunk_of_step(i, n_chunks, n_ctx_chunks, rev):
    if not rev:
        return i
    return jnp.where(i < n_ctx_chunks, n_ctx_chunks - 1 - i, n_chunks - 1 - (i - n_ctx_chunks))


def _tri(n, rev):
    t = lax.broadcasted_iota(jnp.int32, (n, n), 0)
    s = lax.broadcasted_iota(jnp.int32, (n, n), 1)
    return jnp.where((s >= t) if rev else (s <= t), 1.0, 0.0).astype(BF16)


def _round_robin(gens):
    live = list(gens)
    while live:
        still = []
        for g in live:
            try:
                next(g)
                still.append(g)
            except StopIteration:
                pass
        live = still


def _scan_call(chain_fn, p, consts, halo_w, out_w, state_shapes, name):
    bsz, r, pw = p.shape
    ln = SCAN_CHUNK
    nc, ncc = r // ln, CTX // ln
    per8 = ln // 8
    dirs = (False, True)

    def kern(*refs):
        n_halo = 4 if halo_w else 0
        p_refs, halo_refs = refs[0:2], refs[2:2 + n_halo]
        const_refs = refs[2 + n_halo:2 + n_halo + len(consts)]
        o_refs = refs[2 + n_halo + len(consts):4 + n_halo + len(consts)]
        scr = refs[4 + n_halo + len(consts):]
        i = pl.program_id(0)

        @pl.when(i == 0)
        def _():
            for s_ref in scr:
                s_ref[...] = jnp.zeros_like(s_ref)

        gens = []
        for b in range(bsz):
            for d, rev in enumerate(dirs):
                c = (b * 2 + d) * len(state_shapes)
                halos = (halo_refs[2 * d], halo_refs[2 * d + 1]) if halo_w else ()
                gens.append(chain_fn(p_refs[d], *halos, *const_refs, o_refs[d], *scr[c:c + len(state_shapes)],
                                     b=b, rev=rev, chunk=_chunk_of_step(i, nc, ncc, rev),
                                     n_chunks=nc, n_ctx_chunks=ncc))
        _round_robin(gens)

    chunk = lambda rev: (lambda i: _chunk_of_step(i, nc, ncc, rev))
    in_specs = [pl.BlockSpec((bsz, ln, pw), lambda i, f=chunk(rev): (0, f(i), 0)) for rev in dirs]
    args = [p, p]
    if halo_w:
        for rev in dirs:
            f = chunk(rev)
            in_specs.append(pl.BlockSpec((bsz, 8, halo_w), lambda i, f=f: (0, jnp.maximum(f(i) * per8 - 1, 0), 0)))
            in_specs.append(pl.BlockSpec((bsz, 8, halo_w),
                                         lambda i, f=f: (0, jnp.minimum((f(i) + 1) * per8, r // 8 - 1), 0)))
            args += [p, p]
    for cst in consts:
        in_specs.append(pl.BlockSpec(cst.shape, lambda i, n=cst.ndim: (0,) * n))
        args.append(cst)
    return pl.pallas_call(
        kern,
        grid=(nc,),
        in_specs=in_specs,
        out_specs=[pl.BlockSpec((bsz, ln, out_w), lambda i, f=chunk(rev): (0, f(i), 0)) for rev in dirs],
        out_shape=[jax.ShapeDtypeStruct((bsz, r, out_w), F32)] * 2,
        scratch_shapes=[pltpu.VMEM(shp, F32) for _ in range(2 * bsz) for shp in state_shapes],
        compiler_params=_cparams(("arbitrary",)),
        name=name,
    )(*args)


def _mlstm_chain(p_ref, prev_ref, next_ref, cw_ref, cb_ref, gb_ref, o_ref, ct_scr, n_scr, m_scr,
                 *, b, rev, chunk, n_chunks, n_ctx_chunks):
    ln = p_ref.shape[1]
    hw = N_HEADS * B_DH

    x = p_ref[b, :, 0:2 * hw]
    seg_first = (chunk == 0) | (chunk == n_ctx_chunks)
    seg_last = (chunk == n_ctx_chunks - 1) | (chunk == n_chunks - 1)
    row_prev = jnp.where(seg_first, 0.0, prev_ref[b, 7:8, :])
    row_next = jnp.where(seg_last, 0.0, next_ref[b, 0:1, :])
    ridx = lax.broadcasted_iota(jnp.int32, x.shape, 0)
    x_prev = jnp.where(ridx == 0, row_prev, pltpu.roll(x, 1, 0))
    x_next = jnp.where(ridx == ln - 1, row_next, pltpu.roll(x, ln - 1, 0))
    cw = cw_ref[...]
    y = cw[0:1] * x_prev + cw[1:2] * x + cw[2:3] * x_next + cb_ref[...]
    y = y * _sigmoid(y)
    q = y[:, 0:hw]
    k = y[:, hw:2 * hw] * (B_DH ** -0.5)
    v = p_ref[b, :, 2 * hw:3 * hw]

    g = p_ref[b, :, 4 * hw:4 * hw + LANES] + gb_ref[...]
    base = 8 if rev else 0
    end = 0 if rev else ln - 1
    tri = _tri(ln, rev)
    gt8 = g.T[base:base + 8]
    lf_r = _log_sigmoid(gt8)
    b_r = _dot_f32_by_bf16(lf_r, tri, nt=True)
    lf_c = pltpu.roll(_log_sigmoid(g), LANES - N_HEADS, 1)
    c_col = g - _dot_bf16_by_f32(tri, lf_c)
    s_idx = lax.broadcasted_iota(jnp.int32, (ln, ln), 0)
    t_idx = lax.broadcasted_iota(jnp.int32, (ln, ln), 1)
    causal = (s_idx >= t_idx) if rev else (s_idx <= t_idx)

    qb = q.astype(BF16)
    kb = k.astype(BF16)
    vb = v.astype(BF16)
    eye = jnp.where(lax.broadcasted_iota(jnp.int32, (hw, hw), 0) == lax.broadcasted_iota(jnp.int32, (hw, hw), 1),
                    1.0, 0.0).astype(BF16)
    qt = _dot_nt(eye, qb).astype(BF16)
    vt = _dot_nt(eye, vb)
    ct_old = ct_scr[...]
    n_old = n_scr[...]
    m_old = m_scr[0:1, :]
    nq_inter = _dot_f32_by_bf16(n_old, qt)
    rows_hw = lax.broadcasted_iota(jnp.int32, (hw, ln), 0)
    qtm = [jnp.where((rows_hw >= h * B_DH) & (rows_hw < (h + 1) * B_DH), qt, jnp.zeros_like(qt))
           for h in range(N_HEADS)]
    yield
    st = [_dot(kb, qtm[h]) for h in range(N_HEADS)]
    inter = [_dot(ct_old[h * B_DH:(h + 1) * B_DH].astype(BF16), qtm[h]) for h in range(N_HEADS)]
    yield

    h_parts, w_end_rows, decay_cols, decay8, w_end8 = [], [], [], [], []
    m_new_row = jnp.zeros((1, LANES), F32)
    lane128 = _lane_iota((1, LANES))
    for h in range(N_HEADS):
        col = c_col[:, base + h:base + h + 1]
        br_h = b_r[4 + h:5 + h, :]
        li_r = gt8[h:h + 1, :]
        m_h = m_old[:, h:h + 1]
        d = jnp.where(causal, col + br_h, -jnp.inf)
        m_inter = br_h + m_h
        m_t = jnp.maximum(m_inter, jnp.max(d, axis=0, keepdims=True))
        g_inter = jnp.exp(m_inter - m_t)
        s = st[h] * jnp.exp(d - m_t)
        num = _dot(vt[h * B_DH:(h + 1) * B_DH].astype(BF16), s.astype(BF16)) + g_inter * inter[h]
        yield
        nq = jnp.sum(s, axis=0, keepdims=True) + g_inter * nq_inter[h:h + 1]
        den = jnp.maximum(jnp.abs(nq), jnp.exp(-m_t))
        h_parts.append(num / den)
        b_end = br_h[:, end:end + 1]
        g_r = b_end - br_h + li_r
        m_new = jnp.maximum(b_end + m_h, jnp.max(g_r, axis=-1, keepdims=True))
        w_end = jnp.exp(g_r - m_new)
        decay = jnp.exp(b_end + m_h - m_new)
        w_end_rows.append(jnp.broadcast_to(w_end, (B_DH, ln)))
        decay_cols.append(jnp.broadcast_to(decay, (B_DH, 1)))
        w_end8.append(w_end)
        decay8.append(decay)
        m_new_row = jnp.where(lane128 == h, m_new, m_new_row)
    o_ref[b] = jnp.concatenate(h_parts, axis=0).T

    vtw = (vt * jnp.concatenate(w_end_rows, axis=0)).astype(BF16)
    ct_scr[...] = jnp.concatenate(decay_cols, axis=0) * ct_old + _dot(vtw, kb)
    pad4 = [jnp.zeros((1, ln), F32)] * (8 - N_HEADS)
    n_upd = _dot_f32_by_bf16(jnp.concatenate(w_end8 + pad4, axis=0), kb)
    own = _group_matrix(8, 1, hw, B_DH)
    n_scr[...] = (jnp.concatenate(decay8 + [jnp.zeros((1, 1), F32)] * (8 - N_HEADS), axis=0) * n_old
                  + own * n_upd)
    m_scr[0:1, :] = m_new_row


def _mlstm(pb, conv_w, conv_b, gate_b):
    hw = N_HEADS * B_DH
    return _scan_call(_mlstm_chain, pb, (conv_w, conv_b, gate_b), 2 * hw, hw,
                      ((hw, hw), (8, hw), (8, LANES)), "mlstm")


def _gla_chain(p_ref, wg_ref, bg_ref, o_ref, s_scr, *, b, rev, chunk, n_chunks, n_ctx_chunks):
    ln = p_ref.shape[1]
    kw_ = N_HEADS * C_DK
    vw_ = N_HEADS * C_DV
    c = GLA_SUB
    nb = ln // c
    z = 1 if rev else 0

    q = p_ref[b, :, 0:kw_] * (C_DK ** -0.5)
    k = p_ref[b, :, kw_:2 * kw_]
    v = p_ref[b, :, 2 * kw_:2 * kw_ + vw_]
    lr = p_ref[b, :, 2 * kw_ + vw_:2 * kw_ + vw_ + LANES]
    la = _log_sigmoid(_dot(lr, wg_ref[z], HI) + bg_ref[z]) * (1.0 / GLA_TAU)
    bc = _dot_bf16_by_f32(_tri(ln, rev), la)
    vb = v.astype(BF16)
    blk = _group_matrix(kw_, C_DK, vw_, C_DV)
    blk_b = blk.astype(BF16)

    s_old = s_scr[...]
    o_acc = _dot((q * jnp.exp(bc)).astype(BF16), s_old.astype(BF16))
    yield

    t_loc = lax.broadcasted_iota(jnp.int32, (c, kw_), 0)
    off_order = [jnp.where((t_loc >= t) if rev else (t_loc <= t), 0.0, NEG_BIG) for t in range(c)]
    pair_row = lax.broadcasted_iota(jnp.int32, (c, c * c), 0)
    pair_col = lax.broadcasted_iota(jnp.int32, (c, c * c), 1)
    sum_s = jnp.where((pair_col >= pair_row * c) & (pair_col < pair_row * c + c), 1.0, 0.0).astype(BF16)
    pair_sums = []
    for sub in range(nb):
        sl = slice(sub * c, (sub + 1) * c)
        bcb, qb_, kb_ = bc[sl], q[sl], k[sl]
        pieces = [(qb_[t:t + 1] * kb_) * jnp.exp(bcb[t:t + 1] - bcb + off_order[t]) for t in range(c)]
        e = jnp.concatenate(pieces, axis=0)
        pair_sums.append(_dot(e.astype(BF16), blk_b))
        if sub % 4 == 3:
            yield
    diag_parts = []
    for sub in range(nb):
        prod = (pair_sums[sub] * jnp.concatenate([v[sub * c:(sub + 1) * c]] * c, axis=0)).astype(BF16)
        diag_parts.append(_dot(sum_s, prod))
    yield
    o_acc = o_acc + jnp.concatenate(diag_parts, axis=0)

    rmask = lax.broadcasted_iota(jnp.int32, (N_HEADS * c, kw_), 0)
    lmask = _lane_iota((N_HEADS * c, kw_))
    rmask_v = lax.broadcasted_iota(jnp.int32, (N_HEADS * c, vw_), 0)
    lmask_v = _lane_iota((N_HEADS * c, vw_))
    same_k = jnp.zeros((N_HEADS * c, kw_), jnp.bool_)
    same_v = jnp.zeros((N_HEADS * c, vw_), jnp.bool_)
    for h in range(N_HEADS):
        same_k = same_k | ((rmask >= h * c) & (rmask < (h + 1) * c) & (lmask >= h * C_DK) & (lmask < (h + 1) * C_DK))
        same_v = same_v | ((rmask_v >= h * c) & (rmask_v < (h + 1) * c) & (lmask_v >= h * C_DV) & (lmask_v < (h + 1) * C_DV))
    off_full = jnp.zeros((ln, vw_), F32)
    for j in range(nb):
        if rev:
            if j == 0:
                continue
            rows = slice(0, j * c)
            edge = bc[j * c:j * c + 1]
        else:
            if j == nb - 1:
                continue
            rows = slice((j + 1) * c, ln)
            edge = bc[(j + 1) * c - 1:(j + 1) * c]
        sl = slice(j * c, (j + 1) * c)
        qs = q[rows] * jnp.exp(bc[rows] - edge)
        ks = k[sl] * jnp.exp(edge - bc[sl])
        kbd = jnp.where(same_k, jnp.concatenate([ks] * N_HEADS, axis=0), 0.0)
        vbd = jnp.where(same_v, jnp.concatenate([v[sl]] * N_HEADS, axis=0), 0.0)
        a = _dot_nt(qs.astype(BF16), kbd.astype(BF16))
        contrib = _dot(a.astype(BF16), vbd.astype(BF16))
        n_rows = contrib.shape[0]
        pad = jnp.zeros((ln - n_rows, vw_), F32)
        off_full = off_full + (jnp.concatenate([contrib, pad], axis=0) if rev
                               else jnp.concatenate([pad, contrib], axis=0))
        yield
    o_ref[b] = o_acc + off_full

    end = 0 if rev else ln - 1
    b_end = bc[end:end + 1]
    kd = k * jnp.exp(b_end - bc)
    upd = _dot(kd.T.astype(BF16), vb)
    eye = jnp.where(lax.broadcasted_iota(jnp.int32, (kw_, kw_), 0)
                    == lax.broadcasted_iota(jnp.int32, (kw_, kw_), 1), 1.0, 0.0).astype(F32)
    decay_col = jnp.sum(eye * jnp.exp(b_end), axis=-1, keepdims=True)
    s_scr[...] = decay_col * s_old + blk * upd


def _gla(pc, wg_pad, bg):
    return _scan_call(_gla_chain, pc, (wg_pad, bg), 0, N_HEADS * C_DV,
                      ((N_HEADS * C_DK, N_HEADS * C_DV),), "gla")


def _out_kernel(a_ref, bf_ref, bb_ref, bo_ref, cf_ref, cb_ref, d_ref, pg_ref, x_ref, mod_ref,
                gn_ref, w_ref, o_ref):
    gmat = _group_matrix(GROUP_W, 64, GROUP_W, 64, 1.0 / 64)
    gn = gn_ref[...]

    def rms64(hsum, gain):
        ms = _dot(hsum * hsum, gmat, HI)
        return hsum * lax.rsqrt(ms + EPS) * gain

    b_out = rms64(bf_ref[0] + bb_ref[0], gn[0:1]) * _sigmoid(bo_ref[0])
    c_out = rms64(cf_ref[0] + cb_ref[0], gn[1:2])
    y = jnp.concatenate([a_ref[0], b_out, c_out, d_ref[0]], axis=-1)
    gate = pg_ref[0]
    y = y * (gate * _sigmoid(gate))
    upd = _dot(y.astype(BF16), w_ref[...])
    o_ref[0] = x_ref[0] + mod_ref[0, 0][2:3] * upd


def _out_projection(a_o, bf, bb, pb, cf, cb, d_o, pg, xc, msel, gains, w_out_b, with_ctx):
    bsz, r, d = xc.shape
    tm = ROW_TILE
    nct = CTX // tm
    first = 0 if with_ctx else nct
    row = lambda w: pl.BlockSpec((1, tm, w), lambda b, i: (b, i + first, 0))
    att = pl.BlockSpec((1, tm, 256), lambda b, i: (b, i, 0))
    out_rows = r - first * tm
    return pl.pallas_call(
        _out_kernel,
        grid=(bsz, r // tm - first),
        in_specs=[att, row(256), row(256),
                  pl.BlockSpec((1, tm, 256), lambda b, i: (b, i + first, 3)),
                  row(256), row(256), att, row(PG_W), row(d),
                  pl.BlockSpec((1, 1, 3, d), lambda b, i: (b, jnp.where(i + first >= nct, 1, 0), 0, 0)),
                  pl.BlockSpec((8, 256), lambda b, i: (0, 0)),
                  pl.BlockSpec((GROUP_W * 4, d), lambda b, i: (0, 0))],
        out_specs=pl.BlockSpec((1, tm, d), lambda b, i: (b, i, 0)),
        out_shape=jax.ShapeDtypeStruct((bsz, out_rows, d), F32),
        compiler_params=_cparams(("arbitrary", "arbitrary")),
        name="out_projection",
    )(a_o, bf, bb, pb, cf, cb, d_o, pg, xc, msel, gains, w_out_b)


def _pad_cols(w, width):
    return jnp.pad(w, ((0, 0), (0, width - w.shape[1])))


def _layout_w_in(w):
    a = w[:, 0:768]
    bq, bk, bv = w[:, 768:1024], w[:, 1024:1280], w[:, 1280:1536]
    bg, bo = w[:, 1536:1552], w[:, 1552:1808]
    cq, ck, cv, clr = w[:, 1808:1936], w[:, 1936:2064], w[:, 2064:2320], w[:, 2320:2352]
    dd = w[:, 2352:2864]
    gg = w[:, 2864:3888]
    return jnp.concatenate([a, bq, bk, bv, bo, _pad_cols(bg, LANES), cq, ck, cv, _pad_cols(clr, LANES), dd, gg],
                           axis=1)


def _rope_tables(t, hd):
    nq = hd // 4
    inv = ROPE_THETA ** (-jnp.arange(nq, dtype=F32) / nq)
    pos = jnp.arange(t, dtype=jnp.int32)
    rows = (pos // GRID_W).astype(F32)
    cols = (pos % GRID_W).astype(F32)
    lane = jnp.arange(256)
    d = lane % hd
    use_col = (d // (hd // 2)) == 1
    within = d % (hd // 2)
    fidx = within % nq
    first = within < nq
    ang = jnp.where(use_col[None, :], cols[:, None], rows[:, None]) * inv[fidx][None, :]
    cos = jnp.cos(ang)
    sin = jnp.where(first[None, :], -jnp.sin(ang), jnp.sin(ang))
    cos = jnp.concatenate([jnp.ones((CTX, 256), F32), cos], axis=0)
    sin = jnp.concatenate([jnp.zeros((CTX, 256), F32), sin], axis=0)
    return cos, sin


def _tile_lanes(g, width):
    return jnp.tile(g, width // g.shape[0])


def kernel(x, c, ctx, c_ctx, w_mod, b_mod, norm_g, w_in, w_out, a_qn, a_kn, a_lam, a_subln, b_conv_w, b_conv_b, b_gate_b, b_outn, c_wg, c_bg, c_outn, d_qn, d_kn, d_sink):
    bsz, t, d = x.shape
    depth = w_mod.shape[0]
    assert d == D_MODEL and ctx.shape[1] == CTX and t % ROW_TILE == 0 and bsz + 1 <= 8

    s_in = jnp.zeros((8, d), F32).at[:bsz].set(c).at[bsz].set(c_ctx)
    mod = _modulation(s_in, w_mod, b_mod).reshape(depth, 8, 3, d)
    cos_a, sin_a = _rope_tables(t, A_DK)
    cos_d, sin_d = _rope_tables(t, D_HD)
    tabs = (cos_a, sin_a, cos_d, sin_d)

    xc = jnp.concatenate([ctx, x], axis=1)
    for l in range(depth):
        with_ctx = l < depth - 1
        lam_init = 0.8 - 0.6 * math.exp(-0.3 * l)
        msel = jnp.stack([jnp.broadcast_to(mod[l, bsz], (bsz, 3, d)), mod[l, :bsz]], axis=1)
        w_in_p = _layout_w_in(w_in[l]).astype(BF16)
        pa, pb, pc, pd, pg = _in_projection(xc, msel, norm_g[l], w_in_p)

        gains_att = jnp.zeros((8, 256), F32)
        gains_att = gains_att.at[0].set(_tile_lanes(a_qn[l], 256)).at[1].set(_tile_lanes(a_kn[l], 256))
        gains_att = gains_att.at[2].set(_tile_lanes(d_qn[l], 256)).at[3].set(_tile_lanes(d_kn[l], 256))
        qta, ka, vta, qd, kd, vd = _prep_attn(pa, pd, tabs, gains_att)
        a_o = _diff_attention(qta, ka, vta, a_lam[l], a_subln[l], lam_init, with_ctx)
        sink = jnp.zeros((1, LANES), F32).at[0, :N_HEADS].set(d_sink[l])
        d_o = _window_attention(qd, kd, vd, sink, with_ctx)

        gate_b = jnp.zeros((1, LANES), F32).at[0, :16].set(b_gate_b[l].reshape(16))
        conv_b = b_conv_b[l].reshape(1, -1)
        bf, bb = _mlstm(pb, b_conv_w[l], conv_b, gate_b)

        wg_pad = jnp.zeros((2, LANES, LANES), F32)
        for z in range(2):
            wg_pad = wg_pad.at[z, z * GLA_RANK:(z + 1) * GLA_RANK].set(c_wg[l, z])
        cf, cb = _gla(pc, wg_pad, c_bg[l].reshape(2, 1, LANES))

        gains_out = jnp.zeros((8, 256), F32)
        gains_out = gains_out.at[0].set(_tile_lanes(b_outn[l], 256)).at[1].set(_tile_lanes(c_outn[l], 256))
        xc = _out_projection(a_o, bf, bb, pb, cf, cb, d_o, pg, xc, msel, gains_out,
                             w_out[l].astype(BF16), with_ctx)
    return xc
```

```python
import math
from functools import partial

import jax
import jax.numpy as jnp
from jax import lax
from jax.experimental import pallas as pl
from jax.experimental.pallas import tpu as pltpu

F32 = jnp.float32
BF16 = jnp.bfloat16
HI = lax.Precision.HIGHEST

D_MODEL = 1024
CTX = 256
GRID_W = 64
GROUP_W = 256
N_HEADS = 4
A_DK = 32
A_DV = 64
B_DH = 64
C_DK = 32
C_DV = 64
GLA_RANK = 16
GLA_TAU = 16.0
D_HD = 64
WINDOW = 128
ROPE_THETA = 10000.0
EPS = 1e-6

LANES = 128
VMEM_LIMIT = 56 * 1024 * 1024

PA_W = 768
PB_W = 1152
PC_W = 640
PD_W = 512
PG_W = 1024
P_W = PA_W + PB_W + PC_W + PD_W + PG_W

ROW_TILE = 256
SCAN_CHUNK = 128
GLA_SUB = 16
ATT_TQ = 256
ATT_TK = 512
ATT_TILES_PER_STEP = 8
ATT_SUM_LIMIT = 2.0 ** 40

NEG_BIG = -1e30
LOG2E = 1.4426950408889634
A_VT = 80
N_CHAINS = 2 * N_HEADS
ATT_SKEW = 8


def _cparams(sem):
    return pltpu.CompilerParams(dimension_semantics=sem, vmem_limit_bytes=VMEM_LIMIT)


def _log_sigmoid(x):
    return jnp.minimum(x, 0.0) - jnp.log(1.0 + jnp.exp(-jnp.abs(x)))


def _sigmoid(x):
    return 1.0 / (1.0 + jnp.exp(-x))


def _lane_iota(shape):
    return lax.broadcasted_iota(jnp.int32, shape, len(shape) - 1)


def _head_mask(width, per_head, h, rows=1):
    lane = _lane_iota((rows, width))
    return (lane >= h * per_head) & (lane < (h + 1) * per_head)


def _group_matrix(n_rows, rows_per_group, n_cols, cols_per_group, value=1.0):
    r = lax.broadcasted_iota(jnp.int32, (n_rows, n_cols), 0)
    c = lax.broadcasted_iota(jnp.int32, (n_rows, n_cols), 1)
    same = jnp.zeros((n_rows, n_cols), jnp.bool_)
    for g in range(n_rows // rows_per_group):
        same = same | ((r >= g * rows_per_group) & (r < (g + 1) * rows_per_group)
                       & (c >= g * cols_per_group) & (c < (g + 1) * cols_per_group))
    return jnp.where(same, value, 0.0).astype(F32)


def _dot(a, b, precision=None):
    return jnp.dot(a, b, preferred_element_type=F32, precision=precision)


def _split3(x):
    hi = x.astype(BF16)
    r1 = x - hi.astype(F32)
    mid = r1.astype(BF16)
    lo = (r1 - mid.astype(F32)).astype(BF16)
    return hi, mid, lo


def _dot_f32_by_bf16(a, b_bf16, nt=False):
    f = _dot_nt if nt else _dot
    hi, mid, lo = _split3(a)
    return f(hi, b_bf16) + f(mid, b_bf16) + f(lo, b_bf16)


def _dot_bf16_by_f32(a_bf16, b):
    hi, mid, lo = _split3(b)
    return _dot(a_bf16, hi) + _dot(a_bf16, mid) + _dot(a_bf16, lo)


def _dot_nt(a, b, precision=None):
    return lax.dot_general(a, b, (((1,), (1,)), ((), ())), preferred_element_type=F32,
                           precision=precision)


def _mod_kernel(s_ref, w_ref, b_ref, o_ref):
    s = s_ref[...]
    s = s * _sigmoid(s)
    o_ref[0] = _dot(s, w_ref[0], HI) + b_ref[0]


def _modulation(s_in, w_mod, b_mod):
    depth, d, n = w_mod.shape
    tn = 1024
    return pl.pallas_call(
        _mod_kernel,
        grid=(depth, n // tn),
        in_specs=[pl.BlockSpec((8, d), lambda l, j: (0, 0)),
                  pl.BlockSpec((1, d, tn), lambda l, j: (l, 0, j)),
                  pl.BlockSpec((1, 1, tn), lambda l, j: (l, 0, j))],
        out_specs=pl.BlockSpec((1, 8, tn), lambda l, j: (l, 0, j)),
        out_shape=jax.ShapeDtypeStruct((depth, 8, n), F32),
        compiler_params=_cparams(("arbitrary", "arbitrary")),
        name="modulation",
    )(s_in, w_mod, b_mod.reshape(depth, 1, n))


def _stream_specs(stream, tm, first=0):
    nct = CTX // tm
    if isinstance(stream, tuple):
        ctx, lat = stream
        off = nct
    else:
        ctx = lat = stream
        off = 0
    d = lat.shape[-1]
    return ([pl.BlockSpec((1, tm, d), lambda b, i: (b, jnp.minimum(i + first, nct - 1), 0)),
             pl.BlockSpec((1, tm, d), lambda b, i: (b, jnp.maximum(i + first - off, 0), 0))], [ctx, lat])


def _stream_tile(ctx_ref, lat_ref, tile):
    return jnp.where(tile < CTX // ctx_ref.shape[1], ctx_ref[0], lat_ref[0])


def _proj_kernel(xc_ref, xl_ref, mod_ref, g_ref, w_ref, pa_ref, pb_ref, pc_ref, pd_ref, pg_ref):
    x = _stream_tile(xc_ref, xl_ref, pl.program_id(1))
    ms = jnp.mean(x * x, axis=-1, keepdims=True)
    y = x * lax.rsqrt(ms + EPS) * g_ref[...]
    mod = mod_ref[0, 0]
    hx = y * (1.0 + mod[1:2]) + mod[0:1]
    p = _dot(hx.astype(BF16), w_ref[...])
    o = 0
    for ref, w in ((pa_ref, PA_W), (pb_ref, PB_W), (pc_ref, PC_W), (pd_ref, PD_W), (pg_ref, PG_W)):
        ref[0] = p[:, o:o + w]
        o += w


def _in_projection(stream, msel, norm_g, w_in_p):
    bsz, d = msel.shape[0], msel.shape[-1]
    r = sum(a.shape[1] for a in stream) if isinstance(stream, tuple) else stream.shape[1]
    tm = ROW_TILE
    nct = CTX // tm
    row = lambda w: pl.BlockSpec((1, tm, w), lambda b, i: (b, i, 0))
    stream_specs, stream_args = _stream_specs(stream, tm)
    return pl.pallas_call(
        _proj_kernel,
        grid=(bsz, r // tm),
        in_specs=[*stream_specs,
                  pl.BlockSpec((1, 1, 3, d), lambda b, i: (b, jnp.where(i >= nct, 1, 0), 0, 0)),
                  pl.BlockSpec((1, d), lambda b, i: (0, 0)),
                  pl.BlockSpec((d, P_W), lambda b, i: (0, 0))],
        out_specs=[row(PA_W), row(PB_W), row(PC_W), row(PD_W), row(PG_W)],
        out_shape=[jax.ShapeDtypeStruct((bsz, r, w), F32) for w in (PA_W, PB_W, PC_W, PD_W, PG_W)],
        compiler_params=_cparams(("arbitrary", "arbitrary")),
        name="in_projection",
    )(*stream_args, msel, norm_g.reshape(1, d), w_in_p)


def _norm_rope(x, gain, cos, sin, group, half_pair, scale):
    w = x.shape[-1]
    gmat = _group_matrix(w, group, w, group, 1.0 / group).astype(BF16)
    ms = _dot_f32_by_bf16(x * x, gmat)
    xn = x * lax.rsqrt(ms + EPS) * gain
    lane = _lane_iota(xn.shape)
    first = (lane & (2 * half_pair - 1)) < half_pair
    partner = jnp.where(first, pltpu.roll(xn, w - half_pair, 1), pltpu.roll(xn, half_pair, 1))
    out = xn * cos + partner * sin
    return out * scale if scale != 1.0 else out


def _expand_rope(row_ref, col_ref, use_col):
    per_tile = ROW_TILE // GRID_W
    rt = row_ref[0]
    by_row = jnp.concatenate([jnp.broadcast_to(rt[j:j + 1], (GRID_W, 256)) for j in range(per_tile)], axis=0)
    by_col = jnp.concatenate([col_ref[...]] * per_tile, axis=0)
    return jnp.where(use_col, by_col, by_row)


def _prep_attn_kernel(pa_ref, pd_ref, rcosa_ref, rsina_ref, ccosa_ref, csina_ref,
                      rcosd_ref, rsind_ref, ccosd_ref, csind_ref, gains_ref,
                      qta_ref, ka_ref, vta_ref, qdt_ref, kd_ref, vdt_ref):
    pa = pa_ref[0]
    latent = pl.program_id(1) >= CTX // ROW_TILE
    lane = _lane_iota((ROW_TILE, 256))
    col_a = ((lane & (A_DK // 2)) != 0) & latent
    col_d = ((lane & (D_HD // 2)) != 0) & latent
    cosa, sina = _expand_rope(rcosa_ref, ccosa_ref, col_a), _expand_rope(rsina_ref, csina_ref, col_a)
    gains = gains_ref[...]
    qa = _norm_rope(pa[:, 0:256], gains[0:1], cosa, sina, A_DK, A_DK // 4, A_DK ** -0.5 * LOG2E)
    ka = _norm_rope(pa[:, 256:512], gains[1:2], cosa, sina, A_DK, A_DK // 4, 1.0)
    qta_ref[0] = qa.T.astype(BF16)
    ka_ref[0] = ka.astype(BF16)
    vat = pa[:, 512:768].T
    rows = vat.shape[1]
    ones_rows = jnp.where(lax.broadcasted_iota(jnp.int32, (A_VT - A_DV, rows), 0) == 0, 1.0, 0.0).astype(BF16)
    for h in range(N_HEADS):
        vta_ref[0, h, 0:A_DV, :] = vat[h * A_DV:(h + 1) * A_DV].astype(BF16)
        vta_ref[0, h, A_DV:A_VT, :] = ones_rows
    pd = pd_ref[0]
    cosd, sind = _expand_rope(rcosd_ref, ccosd_ref, col_d), _expand_rope(rsind_ref, csind_ref, col_d)
    qd = _norm_rope(pd[:, 0:256], gains[2:3], cosd, sind, D_HD, D_HD // 4, D_HD ** -0.5 * LOG2E)
    kd = _norm_rope(pd[:, 256:384], gains[3:4, 0:128], cosd[:, 0:128], sind[:, 0:128], D_HD, D_HD // 4, 1.0)
    qdt_ref[0] = qd.T.astype(BF16)
    kd_ref[0] = kd.astype(BF16)
    vdt = pd[:, 384:512].T
    for kv in range(2):
        vdt_ref[0, kv, 0:D_HD, :] = vdt[kv * D_HD:(kv + 1) * D_HD].astype(BF16)
        vdt_ref[0, kv, D_HD:A_VT, :] = ones_rows


def _prep_attn(pa, pd, tabs, gains):
    bsz, r, _ = pa.shape
    tr = ROW_TILE
    row = lambda w: pl.BlockSpec((1, tr, w), lambda b, i: (b, i, 0))
    rtab = pl.BlockSpec((1, 8, 256), lambda b, i: (i, 0, 0))
    ctab = pl.BlockSpec((GRID_W, 256), lambda b, i: (0, 0))
    return pl.pallas_call(
        _prep_attn_kernel,
        grid=(bsz, r // tr),
        in_specs=[row(PA_W), row(PD_W), rtab, rtab, ctab, ctab, rtab, rtab, ctab, ctab,
                  pl.BlockSpec((8, 256), lambda b, i: (0, 0))],
        out_specs=[pl.BlockSpec((1, 256, tr), lambda b, i: (b, 0, i)),
                   row(256),
                   pl.BlockSpec((1, N_HEADS, A_VT, tr), lambda b, i: (b, 0, 0, i)),
                   pl.BlockSpec((1, 256, tr), lambda b, i: (b, 0, i)),
                   row(128),
                   pl.BlockSpec((1, 2, A_VT, tr), lambda b, i: (b, 0, 0, i))],
        out_shape=[jax.ShapeDtypeStruct((bsz, 256, r), BF16),
                   jax.ShapeDtypeStruct((bsz, r, 256), BF16),
                   jax.ShapeDtypeStruct((bsz, N_HEADS, A_VT, r), BF16),
                   jax.ShapeDtypeStruct((bsz, 256, r), BF16),
                   jax.ShapeDtypeStruct((bsz, r, 128), BF16),
                   jax.ShapeDtypeStruct((bsz, 2, A_VT, r), BF16)],
        compiler_params=_cparams(("arbitrary", "arbitrary")),
        name="prep_attn",
    )(pa, pd, *tabs, gains)


def _diff_attn_kernel(qt_ref, k_ref, vt_ref, lam_ref, g_ref, o_ref, *scr, first_block, lam_init):
    wq_scr, acc_scr, m_scr = scr[0:N_CHAINS], scr[N_CHAINS:2 * N_CHAINS], scr[2 * N_CHAINS:3 * N_CHAINS]
    i = pl.program_id(1) + first_block
    tq = qt_ref.shape[2]
    n_tiles = (k_ref.shape[1] - CTX) // ATT_TK
    tiles_per_step = min(ATT_TILES_PER_STEP, n_tiles)
    step_keys = ATT_TK * tiles_per_step
    n_steps = jnp.where(i == 0, 0, n_tiles // tiles_per_step)
    row = lax.broadcasted_iota(jnp.int32, (LANES, tq), 0)
    for ch in range(N_CHAINS):
        grp, r_in = ch // 4, (ch % 4) * A_DK
        qt = qt_ref[0, grp * LANES:(grp + 1) * LANES, :]
        wq_scr[ch][...] = jnp.where((row >= r_in) & (row < r_in + A_DK), qt, jnp.zeros_like(qt))
        m_scr[ch][...] = jnp.full((8, tq), NEG_BIG, F32)
        acc_scr[ch][...] = jnp.zeros((A_VT, tq), F32)

    def process(tiles):
        units = [(ch, off, size) for off, size in tiles for ch in range(N_CHAINS)]

        def scores(u):
            ch, off, size = units[u]
            grp = ch // 4
            kt = k_ref[0, pl.ds(off, size), grp * LANES:(grp + 1) * LANES]
            return _dot(kt, wq_scr[ch][...])

        s = {u: scores(u) for u in range(min(ATT_SKEW, len(units)))}
        for u, (ch, off, size) in enumerate(units):
            m_old = m_scr[ch][0:1, :]
            m_new = jnp.maximum(m_old, jnp.max(s[u], axis=0, keepdims=True))
            alpha = jnp.exp2(m_old - m_new)
            p = jnp.exp2(s.pop(u) - m_new).astype(BF16)
            if u + ATT_SKEW < len(units):
                s[u + ATT_SKEW] = scores(u + ATT_SKEW)
            pv = _dot(vt_ref[0, ch // 2, :, pl.ds(off, size)], p)
            acc_scr[ch][...] = alpha * acc_scr[ch][...] + pv
            m_scr[ch][...] = jnp.broadcast_to(m_new, (8, tq))

    def process_fixed_max(tiles):
        units = [(ch, off, size) for off, size in tiles for ch in range(N_CHAINS)]

        def scores(u):
            ch, off, size = units[u]
            grp = ch // 4
            kt = k_ref[0, pl.ds(off, size), grp * LANES:(grp + 1) * LANES]
            return _dot(kt, wq_scr[ch][...])

        s = {u: scores(u) for u in range(min(ATT_SKEW, len(units)))}
        pv = [None] * N_CHAINS
        for u, (ch, off, size) in enumerate(units):
            p = jnp.exp2(s.pop(u) - m_scr[ch][0:1, :]).astype(BF16)
            if u + ATT_SKEW < len(units):
                s[u + ATT_SKEW] = scores(u + ATT_SKEW)
            d = _dot(vt_ref[0, ch // 2, :, pl.ds(off, size)], p)
            pv[ch] = d if pv[ch] is None else pv[ch] + d
        worst = None
        for ch in range(N_CHAINS):
            row_sum = acc_scr[ch][A_DV:A_DV + 1, :] + pv[ch][A_DV:A_DV + 1, :]
            worst = row_sum if worst is None else jnp.maximum(worst, row_sum)
        ok = jnp.max(worst) < ATT_SUM_LIMIT

        def commit():
            for ch in range(N_CHAINS):
                acc_scr[ch][...] = acc_scr[ch][...] + pv[ch]

        lax.cond(ok, commit, lambda: process_tiles(tiles[0][0], len(tiles)))

    def tile_offset(first, t):
        return pl.multiple_of(first + t * ATT_TK, math.gcd(CTX, ATT_TK))

    def process_tiles(first, n):
        def one(t, carry):
            process([(tile_offset(first, t), ATT_TK)])
            return carry
        lax.fori_loop(0, n, one, 0)

    process([(0, CTX)])

    def step(j, carry):
        process_fixed_max([(tile_offset(CTX + j * step_keys, t), ATT_TK) for t in range(tiles_per_step)])
        return carry

    lax.fori_loop(0, n_steps, step, 0)
    process_tiles(CTX + n_steps * step_keys, jnp.where(i == 0, 0, n_tiles % tiles_per_step))

    lp = lam_ref[...]
    lam = (jnp.exp(jnp.sum(lp[0:1] * lp[1:2], axis=-1, keepdims=True))
           - jnp.exp(jnp.sum(lp[2:3] * lp[3:4], axis=-1, keepdims=True)) + lam_init)
    outs = []
    for h in range(N_HEADS):
        comp = []
        for c in range(2):
            acc = acc_scr[2 * h + c][...]
            comp.append(acc[0:A_DV] / acc[A_DV:A_DV + 1])
        o = comp[0] - lam * comp[1]
        ms = jnp.mean(o * o, axis=0, keepdims=True)
        outs.append(o * lax.rsqrt(ms + EPS) * g_ref[...] * (1.0 - lam_init))
    o_ref[0] = jnp.concatenate(outs, axis=0).T


def _diff_attention(qta, ka, vta, a_lam, a_subln, lam_init, with_ctx):
    bsz, r, _ = ka.shape
    tq = ATT_TQ
    first = 0 if with_ctx else CTX // tq
    nq = r // tq - first
    return pl.pallas_call(
        partial(_diff_attn_kernel, first_block=first, lam_init=lam_init),
        grid=(bsz, nq),
        in_specs=[pl.BlockSpec((1, 256, tq), lambda b, i: (b, 0, i + first)),
                  pl.BlockSpec((1, r, 256), lambda b, i: (b, 0, 0)),
                  pl.BlockSpec((1, N_HEADS, A_VT, r), lambda b, i: (b, 0, 0, 0)),
                  pl.BlockSpec((4, A_DK), lambda b, i: (0, 0)),
                  pl.BlockSpec((A_DV, 1), lambda b, i: (0, 0))],
        out_specs=pl.BlockSpec((1, tq, 256), lambda b, i: (b, i, 0)),
        out_shape=jax.ShapeDtypeStruct((bsz, nq * tq, 256), F32),
        scratch_shapes=([pltpu.VMEM((LANES, tq), BF16)] * N_CHAINS
                        + [pltpu.VMEM((A_VT, tq), F32)] * N_CHAINS
                        + [pltpu.VMEM((8, tq), F32)] * N_CHAINS),
        compiler_params=_cparams(("arbitrary", "arbitrary")),
        name="diff_attention",
    )(qta, ka, vta, a_lam, a_subln.reshape(A_DV, 1))


def _window_attn_kernel(q_ref, k_ref, vt_ref, sink_ref, o_ref, *, first_block):
    i = pl.program_id(1) + first_block
    tq = q_ref.shape[2]
    r = k_ref.shape[1]
    span = tq + 2 * WINDOW
    start = jnp.clip(i * tq - WINDOW, CTX, r - span)
    start = pl.multiple_of(start, WINDOW)
    kwin = k_ref[0, pl.ds(start, span), :]
    kctx = k_ref[0, 0:CTX, :]
    kpos = start + lax.broadcasted_iota(jnp.int32, (span, tq), 0)
    qpos = i * tq + lax.broadcasted_iota(jnp.int32, (span, tq), 1)
    valid = (jnp.abs(kpos - qpos) <= WINDOW) & (qpos >= CTX)
    sink = sink_ref[...] * LOG2E
    zeros = jnp.zeros((D_HD, tq), BF16)
    scores = []
    for h in range(N_HEADS):
        qh = q_ref[0, h * D_HD:(h + 1) * D_HD, :]
        wq = jnp.concatenate([qh, zeros] if h // 2 == 0 else [zeros, qh], axis=0)
        scores.append((_dot(kwin, wq), _dot(kctx, wq)))
    outs = []
    for h in range(N_HEADS):
        kv = h // 2
        s_loc = jnp.where(valid, scores[h][0], -jnp.inf)
        s_ctx = scores[h][1]
        sk = sink[0:1, h:h + 1]
        m = jnp.maximum(jnp.maximum(jnp.max(s_loc, axis=0, keepdims=True),
                                    jnp.max(s_ctx, axis=0, keepdims=True)), sk)
        p_loc = jnp.exp2(s_loc - m).astype(BF16)
        p_ctx = jnp.exp2(s_ctx - m).astype(BF16)
        acc = (_dot(vt_ref[0, kv, :, pl.ds(start, span)], p_loc)
               + _dot(vt_ref[0, kv, :, 0:CTX], p_ctx))
        outs.append(acc[0:D_HD] / (acc[D_HD:D_HD + 1] + jnp.exp2(sk - m)))
    o_ref[0] = jnp.concatenate(outs, axis=0).T


def _window_attention(qdt, kd, vdt, sink, with_ctx):
    bsz, r, _ = kd.shape
    tq = ROW_TILE
    first = 0 if with_ctx else CTX // tq
    return pl.pallas_call(
        partial(_window_attn_kernel, first_block=first),
        grid=(bsz, r // tq - first),
        in_specs=[pl.BlockSpec((1, 256, tq), lambda b, i: (b, 0, i + first)),
                  pl.BlockSpec((1, r, 128), lambda b, i: (b, 0, 0)),
                  pl.BlockSpec((1, 2, A_VT, r), lambda b, i: (b, 0, 0, 0)),
                  pl.BlockSpec((1, LANES), lambda b, i: (0, 0))],
        out_specs=pl.BlockSpec((1, tq, 256), lambda b, i: (b, i, 0)),
        out_shape=jax.ShapeDtypeStruct((bsz, r - first * tq, 256), F32),
        compiler_params=_cparams(("arbitrary", "arbitrary")),
        name="window_attention",
    )(qdt, kd, vdt, sink)


def _chunk_of_step(i, n_chunks, n_ctx_chunks, rev):
    if not rev:
        return i
    return jnp.where(i < n_ctx_chunks, n_ctx_chunks - 1 - i, n_chunks - 1 - (i - n_ctx_chunks))


def _tri(n, rev):
    t = lax.broadcasted_iota(jnp.int32, (n, n), 0)
    s = lax.broadcasted_iota(jnp.int32, (n, n), 1)
    return jnp.where((s >= t) if rev else (s <= t), 1.0, 0.0).astype(BF16)


def _round_robin(gens):
    live = list(gens)
    while live:
        still = []
        for g in live:
            try:
                next(g)
                still.append(g)
            except StopIteration:
                pass
        live = still


def _scan_call(chain_fn, p, consts, halo_w, out_w, state_shapes, name):
    bsz, r, pw = p.shape
    ln = SCAN_CHUNK
    nc, ncc = r // ln, CTX // ln
    per8 = ln // 8
    dirs = (False, True)

    def kern(*refs):
        n_halo = 4 if halo_w else 0
        p_refs, halo_refs = refs[0:2], refs[2:2 + n_halo]
        const_refs = refs[2 + n_halo:2 + n_halo + len(consts)]
        o_refs = refs[2 + n_halo + len(consts):4 + n_halo + len(consts)]
        scr = refs[4 + n_halo + len(consts):]
        i = pl.program_id(0)

        @pl.when(i == 0)
        def _():
            for s_ref in scr:
                s_ref[...] = jnp.zeros_like(s_ref)

        gens = []
        for b in range(bsz):
            for d, rev in enumerate(dirs):
                c = (b * 2 + d) * len(state_shapes)
                halos = (halo_refs[2 * d], halo_refs[2 * d + 1]) if halo_w else ()
                gens.append(chain_fn(p_refs[d], *halos, *const_refs, o_refs[d], *scr[c:c + len(state_shapes)],
                                     b=b, rev=rev, chunk=_chunk_of_step(i, nc, ncc, rev),
                                     n_chunks=nc, n_ctx_chunks=ncc))
        _round_robin(gens)

    chunk = lambda rev: (lambda i: _chunk_of_step(i, nc, ncc, rev))
    in_specs = [pl.BlockSpec((bsz, ln, pw), lambda i, f=chunk(rev): (0, f(i), 0)) for rev in dirs]
    args = [p, p]
    if halo_w:
        for rev in dirs:
            f = chunk(rev)
            in_specs.append(pl.BlockSpec((bsz, 8, halo_w), lambda i, f=f: (0, jnp.maximum(f(i) * per8 - 1, 0), 0)))
            in_specs.append(pl.BlockSpec((bsz, 8, halo_w),
                                         lambda i, f=f: (0, jnp.minimum((f(i) + 1) * per8, r // 8 - 1), 0)))
            args += [p, p]
    for cst in consts:
        in_specs.append(pl.BlockSpec(cst.shape, lambda i, n=cst.ndim: (0,) * n))
        args.append(cst)
    return pl.pallas_call(
        kern,
        grid=(nc,),
        in_specs=in_specs,
        out_specs=[pl.BlockSpec((bsz, ln, out_w), lambda i, f=chunk(rev): (0, f(i), 0)) for rev in dirs],
        out_shape=[jax.ShapeDtypeStruct((bsz, r, out_w), F32)] * 2,
        scratch_shapes=[pltpu.VMEM(shp, F32) for _ in range(2 * bsz) for shp in state_shapes],
        compiler_params=_cparams(("arbitrary",)),
        name=name,
    )(*args)


def _mlstm_chain(p_ref, prev_ref, next_ref, cw_ref, cb_ref, gb_ref, o_ref, ct_scr, n_scr, m_scr,
                 *, b, rev, chunk, n_chunks, n_ctx_chunks):
    ln = p_ref.shape[1]
    hw = N_HEADS * B_DH

    x = p_ref[b, :, 0:2 * hw]
    seg_first = (chunk == 0) | (chunk == n_ctx_chunks)
    seg_last = (chunk == n_ctx_chunks - 1) | (chunk == n_chunks - 1)
    row_prev = jnp.where(seg_first, 0.0, prev_ref[b, 7:8, :])
    row_next = jnp.where(seg_last, 0.0, next_ref[b, 0:1, :])
    ridx = lax.broadcasted_iota(jnp.int32, x.shape, 0)
    x_prev = jnp.where(ridx == 0, row_prev, pltpu.roll(x, 1, 0))
    x_next = jnp.where(ridx == ln - 1, row_next, pltpu.roll(x, ln - 1, 0))
    cw = cw_ref[...]
    y = cw[0:1] * x_prev + cw[1:2] * x + cw[2:3] * x_next + cb_ref[...]
    y = y * _sigmoid(y)
    q = y[:, 0:hw]
    k = y[:, hw:2 * hw] * (B_DH ** -0.5)
    v = p_ref[b, :, 2 * hw:3 * hw]

    g = p_ref[b, :, 4 * hw:4 * hw + LANES] + gb_ref[...]
    base = 8 if rev else 0
    end = 0 if rev else ln - 1
    tri = _tri(ln, rev)
    gt8 = g.T[base:base + 8]
    lf_r = _log_sigmoid(gt8)
    b_r = _dot_f32_by_bf16(lf_r, tri, nt=True)
    lf_c = pltpu.roll(_log_sigmoid(g), LANES - N_HEADS, 1)
    c_col = g - _dot_bf16_by_f32(tri, lf_c)
    s_idx = lax.broadcasted_iota(jnp.int32, (ln, ln), 0)
    t_idx = lax.broadcasted_iota(jnp.int32, (ln, ln), 1)
    causal = (s_idx >= t_idx) if rev else (s_idx <= t_idx)

    qb = q.astype(BF16)
    kb = k.astype(BF16)
    vb = v.astype(BF16)
    eye = jnp.where(lax.broadcasted_iota(jnp.int32, (hw, hw), 0) == lax.broadcasted_iota(jnp.int32, (hw, hw), 1),
                    1.0, 0.0).astype(BF16)
    qt = _dot_nt(eye, qb).astype(BF16)
    vt = _dot_nt(eye, vb)
    ct_old = ct_scr[...]
    n_old = n_scr[...]
    m_old = m_scr[0:1, :]
    nq_inter = _dot_f32_by_bf16(n_old, qt)
    rows_hw = lax.broadcasted_iota(jnp.int32, (hw, ln), 0)
    qtm = [jnp.where((rows_hw >= h * B_DH) & (rows_hw < (h + 1) * B_DH), qt, jnp.zeros_like(qt))
           for h in range(N_HEADS)]
    yield
    st = [_dot(kb, qtm[h]) for h in range(N_HEADS)]
    inter = [_dot(ct_old[h * B_DH:(h + 1) * B_DH].astype(BF16), qtm[h]) for h in range(N_HEADS)]
    yield

    h_parts, w_end_rows, decay_cols, decay8, w_end8 = [], [], [], [], []
    m_new_row = jnp.zeros((1, LANES), F32)
    lane128 = _lane_iota((1, LANES))
    for h in range(N_HEADS):
        col = c_col[:, base + h:base + h + 1]
        br_h = b_r[4 + h:5 + h, :]
        li_r = gt8[h:h + 1, :]
        m_h = m_old[:, h:h + 1]
        d = jnp.where(causal, col + br_h, -jnp.inf)
        m_inter = br_h + m_h
        m_t = jnp.maximum(m_inter, jnp.max(d, axis=0, keepdims=True))
        g_inter = jnp.exp(m_inter - m_t)
        s = st[h] * jnp.exp(d - m_t)
        num = _dot(vt[h * B_DH:(h + 1) * B_DH].astype(BF16), s.astype(BF16)) + g_inter * inter[h]
        yield
        nq = jnp.sum(s, axis=0, keepdims=True) + g_inter * nq_inter[h:h + 1]
        den = jnp.maximum(jnp.abs(nq), jnp.exp(-m_t))
        h_parts.append(num / den)
        b_end = br_h[:, end:end + 1]
        g_r = b_end - br_h + li_r
        m_new = jnp.maximum(b_end + m_h, jnp.max(g_r, axis=-1, keepdims=True))
        w_end = jnp.exp(g_r - m_new)
        decay = jnp.exp(b_end + m_h - m_new)
        w_end_rows.append(jnp.broadcast_to(w_end, (B_DH, ln)))
        decay_cols.append(jnp.broadcast_to(decay, (B_DH, 1)))
        w_end8.append(w_end)
        decay8.append(decay)
        m_new_row = jnp.where(lane128 == h, m_new, m_new_row)
    o_ref[b] = jnp.concatenate(h_parts, axis=0).T

    vtw = (vt * jnp.concatenate(w_end_rows, axis=0)).astype(BF16)
    ct_scr[...] = jnp.concatenate(decay_cols, axis=0) * ct_old + _dot(vtw, kb)
    pad4 = [jnp.zeros((1, ln), F32)] * (8 - N_HEADS)
    n_upd = _dot_f32_by_bf16(jnp.concatenate(w_end8 + pad4, axis=0), kb)
    own = _group_matrix(8, 1, hw, B_DH)
    n_scr[...] = (jnp.concatenate(decay8 + [jnp.zeros((1, 1), F32)] * (8 - N_HEADS), axis=0) * n_old
                  + own * n_upd)
    m_scr[0:1, :] = m_new_row


def _mlstm(pb, conv_w, conv_b, gate_b):
    hw = N_HEADS * B_DH
    return _scan_call(_mlstm_chain, pb, (conv_w, conv_b, gate_b), 2 * hw, hw,
                      ((hw, hw), (8, hw), (8, LANES)), "mlstm")


def _gla_chain(p_ref, wg_ref, bg_ref, o_ref, s_scr, *, b, rev, chunk, n_chunks, n_ctx_chunks):
    ln = p_ref.shape[1]
    kw_ = N_HEADS * C_DK
    vw_ = N_HEADS * C_DV
    c = GLA_SUB
    nb = ln // c
    z = 1 if rev else 0

    q = p_ref[b, :, 0:kw_] * (C_DK ** -0.5)
    k = p_ref[b, :, kw_:2 * kw_]
    v = p_ref[b, :, 2 * kw_:2 * kw_ + vw_]
    lr = p_ref[b, :, 2 * kw_ + vw_:2 * kw_ + vw_ + LANES]
    la = _log_sigmoid(_dot(lr, wg_ref[z], HI) + bg_ref[z]) * (1.0 / GLA_TAU)
    bc = _dot_bf16_by_f32(_tri(ln, rev), la)
    vb = v.astype(BF16)
    blk = _group_matrix(kw_, C_DK, vw_, C_DV)
    blk_b = blk.astype(BF16)

    s_old = s_scr[...]
    o_acc = _dot((q * jnp.exp(bc)).astype(BF16), s_old.astype(BF16))
    yield

    t_loc = lax.broadcasted_iota(jnp.int32, (c, kw_), 0)
    off_order = [jnp.where((t_loc >= t) if rev else (t_loc <= t), 0.0, NEG_BIG) for t in range(c)]
    pair_row = lax.broadcasted_iota(jnp.int32, (c, c * c), 0)
    pair_col = lax.broadcasted_iota(jnp.int32, (c, c * c), 1)
    sum_s = jnp.where((pair_col >= pair_row * c) & (pair_col < pair_row * c + c), 1.0, 0.0).astype(BF16)
    pair_sums = []
    for sub in range(nb):
        sl = slice(sub * c, (sub + 1) * c)
        bcb, qb_, kb_ = bc[sl], q[sl], k[sl]
        pieces = [(qb_[t:t + 1] * kb_) * jnp.exp(bcb[t:t + 1] - bcb + off_order[t]) for t in range(c)]
        e = jnp.concatenate(pieces, axis=0)
        pair_sums.append(_dot(e.astype(BF16), blk_b))
        if sub % 4 == 3:
            yield
    diag_parts = []
    for sub in range(nb):
        prod = (pair_sums[sub] * jnp.concatenate([v[sub * c:(sub + 1) * c]] * c, axis=0)).astype(BF16)
        diag_parts.append(_dot(sum_s, prod))
    yield
    o_acc = o_acc + jnp.concatenate(diag_parts, axis=0)

    rmask = lax.broadcasted_iota(jnp.int32, (N_HEADS * c, kw_), 0)
    lmask = _lane_iota((N_HEADS * c, kw_))
    rmask_v = lax.broadcasted_iota(jnp.int32, (N_HEADS * c, vw_), 0)
    lmask_v = _lane_iota((N_HEADS * c, vw_))
    same_k = jnp.zeros((N_HEADS * c, kw_), jnp.bool_)
    same_v = jnp.zeros((N_HEADS * c, vw_), jnp.bool_)
    for h in range(N_HEADS):
        same_k = same_k | ((rmask >= h * c) & (rmask < (h + 1) * c) & (lmask >= h * C_DK) & (lmask < (h + 1) * C_DK))
        same_v = same_v | ((rmask_v >= h * c) & (rmask_v < (h + 1) * c) & (lmask_v >= h * C_DV) & (lmask_v < (h + 1) * C_DV))
    off_full = jnp.zeros((ln, vw_), F32)
    for j in range(nb):
        if rev:
            if j == 0:
                continue
            rows = slice(0, j * c)
            edge = bc[j * c:j * c + 1]
        else:
            if j == nb - 1:
                continue
            rows = slice((j + 1) * c, ln)
            edge = bc[(j + 1) * c - 1:(j + 1) * c]
        sl = slice(j * c, (j + 1) * c)
        qs = q[rows] * jnp.exp(bc[rows] - edge)
        ks = k[sl] * jnp.exp(edge - bc[sl])
        kbd = jnp.where(same_k, jnp.concatenate([ks] * N_HEADS, axis=0), 0.0)
        vbd = jnp.where(same_v, jnp.concatenate([v[sl]] * N_HEADS, axis=0), 0.0)
        a = _dot_nt(qs.astype(BF16), kbd.astype(BF16))
        contrib = _dot(a.astype(BF16), vbd.astype(BF16))
        n_rows = contrib.shape[0]
        pad = jnp.zeros((ln - n_rows, vw_), F32)
        off_full = off_full + (jnp.concatenate([contrib, pad], axis=0) if rev
                               else jnp.concatenate([pad, contrib], axis=0))
        yield
    o_ref[b] = o_acc + off_full

    end = 0 if rev else ln - 1
    b_end = bc[end:end + 1]
    kd = k * jnp.exp(b_end - bc)
    upd = _dot(kd.T.astype(BF16), vb)
    eye = jnp.where(lax.broadcasted_iota(jnp.int32, (kw_, kw_), 0)
                    == lax.broadcasted_iota(jnp.int32, (kw_, kw_), 1), 1.0, 0.0).astype(F32)
    decay_col = jnp.sum(eye * jnp.exp(b_end), axis=-1, keepdims=True)
    s_scr[...] = decay_col * s_old + blk * upd


def _gla(pc, wg_pad, bg):
    return _scan_call(_gla_chain, pc, (wg_pad, bg), 0, N_HEADS * C_DV,
                      ((N_HEADS * C_DK, N_HEADS * C_DV),), "gla")


def _out_kernel(a_ref, bf_ref, bb_ref, bo_ref, cf_ref, cb_ref, d_ref, pg_ref, xc_ref, xl_ref, mod_ref,
                gn_ref, w_ref, o_ref, *, first_block):
    gmat = _group_matrix(GROUP_W, 64, GROUP_W, 64, 1.0 / 64).astype(BF16)
    gn = gn_ref[...]

    def rms64(hsum, gain):
        ms = _dot_f32_by_bf16(hsum * hsum, gmat)
        return hsum * lax.rsqrt(ms + EPS) * gain

    b_out = rms64(bf_ref[0] + bb_ref[0], gn[0:1]) * _sigmoid(bo_ref[0])
    c_out = rms64(cf_ref[0] + cb_ref[0], gn[1:2])
    y = jnp.concatenate([a_ref[0], b_out, c_out, d_ref[0]], axis=-1)
    gate = pg_ref[0]
    y = y * (gate * _sigmoid(gate))
    upd = _dot(y.astype(BF16), w_ref[...])
    x = _stream_tile(xc_ref, xl_ref, pl.program_id(1) + first_block)
    o_ref[0] = x + mod_ref[0, 0][2:3] * upd


def _out_projection(a_o, bf, bb, pb, cf, cb, d_o, pg, stream, msel, gains, w_out_b, with_ctx):
    bsz, r, d = pg.shape[0], pg.shape[1], msel.shape[-1]
    tm = ROW_TILE
    nct = CTX // tm
    first = 0 if with_ctx else nct
    row = lambda w: pl.BlockSpec((1, tm, w), lambda b, i: (b, i + first, 0))
    att = pl.BlockSpec((1, tm, 256), lambda b, i: (b, i, 0))
    out_rows = r - first * tm
    stream_specs, stream_args = _stream_specs(stream, tm, first)
    return pl.pallas_call(
        partial(_out_kernel, first_block=first),
        grid=(bsz, r // tm - first),
        in_specs=[att, row(256), row(256),
                  pl.BlockSpec((1, tm, 256), lambda b, i: (b, i + first, 3)),
                  row(256), row(256), att, row(PG_W), *stream_specs,
                  pl.BlockSpec((1, 1, 3, d), lambda b, i: (b, jnp.where(i + first >= nct, 1, 0), 0, 0)),
                  pl.BlockSpec((8, 256), lambda b, i: (0, 0)),
                  pl.BlockSpec((GROUP_W * 4, d), lambda b, i: (0, 0))],
        out_specs=pl.BlockSpec((1, tm, d), lambda b, i: (b, i, 0)),
        out_shape=jax.ShapeDtypeStruct((bsz, out_rows, d), F32),
        compiler_params=_cparams(("arbitrary", "arbitrary")),
        name="out_projection",
    )(a_o, bf, bb, pb, cf, cb, d_o, pg, *stream_args, msel, gains, w_out_b)


def _pad_cols(w, width):
    return jnp.pad(w, ((0, 0), (0, width - w.shape[1])))


def _layout_w_in(w):
    a = w[:, 0:768]
    bq, bk, bv = w[:, 768:1024], w[:, 1024:1280], w[:, 1280:1536]
    bg, bo = w[:, 1536:1552], w[:, 1552:1808]
    cq, ck, cv, clr = w[:, 1808:1936], w[:, 1936:2064], w[:, 2064:2320], w[:, 2320:2352]
    dd = w[:, 2352:2864]
    gg = w[:, 2864:3888]
    return jnp.concatenate([a, bq, bk, bv, bo, _pad_cols(bg, LANES), cq, ck, cv, _pad_cols(clr, LANES), dd, gg],
                           axis=1)


def _rope_tables(t, hd):
    nq = hd // 4
    per_tile = ROW_TILE // GRID_W
    inv = ROPE_THETA ** (-jnp.arange(nq, dtype=F32) / nq)
    lane = jnp.arange(256)
    within = (lane % hd) % (hd // 2)
    freq = inv[within % nq][None, :]
    sign = jnp.where(within < nq, -1.0, 1.0)[None, :]

    def tables(pos):
        ang = pos.astype(F32)[:, None] * freq
        return jnp.cos(ang), sign * jnp.sin(ang)

    row_cos, row_sin = tables(jnp.arange(t // GRID_W))
    col_cos, col_sin = tables(jnp.arange(GRID_W))

    def per_tile_blocks(tab, ctx_value):
        tab = jnp.pad(tab.reshape(-1, per_tile, 256), ((0, 0), (0, 8 - per_tile), (0, 0)))
        return jnp.concatenate([jnp.full((CTX // ROW_TILE, 8, 256), ctx_value, F32), tab], axis=0)

    return per_tile_blocks(row_cos, 1.0), per_tile_blocks(row_sin, 0.0), col_cos, col_sin


def _tile_lanes(g, width):
    return jnp.tile(g, width // g.shape[0])


def kernel(x, c, ctx, c_ctx, w_mod, b_mod, norm_g, w_in, w_out, a_qn, a_kn, a_lam, a_subln, b_conv_w, b_conv_b, b_gate_b, b_outn, c_wg, c_bg, c_outn, d_qn, d_kn, d_sink):
    bsz, t, d = x.shape
    depth = w_mod.shape[0]
    assert d == D_MODEL and ctx.shape[1] == CTX and t % ROW_TILE == 0 and bsz + 1 <= 8

    s_in = jnp.zeros((8, d), F32).at[:bsz].set(c).at[bsz].set(c_ctx)
    mod = _modulation(s_in, w_mod, b_mod).reshape(depth, 8, 3, d)
    tabs = _rope_tables(t, A_DK) + _rope_tables(t, D_HD)

    xc = (ctx, x)
    for l in range(depth):
        with_ctx = l < depth - 1
        lam_init = 0.8 - 0.6 * math.exp(-0.3 * l)
        msel = jnp.stack([jnp.broadcast_to(mod[l, bsz], (bsz, 3, d)), mod[l, :bsz]], axis=1)
        w_in_p = _layout_w_in(w_in[l]).astype(BF16)
        pa, pb, pc, pd, pg = _in_projection(xc, msel, norm_g[l], w_in_p)

        gains_att = jnp.zeros((8, 256), F32)
        gains_att = gains_att.at[0].set(_tile_lanes(a_qn[l], 256)).at[1].set(_tile_lanes(a_kn[l], 256))
        gains_att = gains_att.at[2].set(_tile_lanes(d_qn[l], 256)).at[3].set(_tile_lanes(d_kn[l], 256))
        qta, ka, vta, qdt, kd, vdt = _prep_attn(pa, pd, tabs, gains_att)
        a_o = _diff_attention(qta, ka, vta, a_lam[l], a_subln[l], lam_init, with_ctx)
        sink = jnp.zeros((1, LANES), F32).at[0, :N_HEADS].set(d_sink[l])
        d_o = _window_attention(qdt, kd, vdt, sink, with_ctx)

        gate_b = jnp.zeros((1, LANES), F32).at[0, :16].set(b_gate_b[l].reshape(16))
        conv_b = b_conv_b[l].reshape(1, -1)
        bf, bb = _mlstm(pb, b_conv_w[l], conv_b, gate_b)

        wg_pad = jnp.zeros((2, LANES, LANES), F32)
        for z in range(2):
            wg_pad = wg_pad.at[z, z * GLA_RANK:(z + 1) * GLA_RANK].set(c_wg[l, z])
        cf, cb = _gla(pc, wg_pad, c_bg[l].reshape(2, 1, LANES))

        gains_out = jnp.zeros((8, 256), F32)
        gains_out = gains_out.at[0].set(_tile_lanes(b_outn[l], 256)).at[1].set(_tile_lanes(c_outn[l], 256))
        xc = _out_projection(a_o, bf, bb, pb, cf, cb, d_o, pg, xc, msel, gains_out,
                             w_out[l].astype(BF16), with_ctx)
    return xc
```

```python
import math
from functools import partial

import jax
import jax.numpy as jnp
from jax import lax
from jax.experimental import pallas as pl
from jax.experimental.pallas import tpu as pltpu

F32 = jnp.float32
BF16 = jnp.bfloat16
HI = lax.Precision.HIGHEST

D_MODEL = 1024
CTX = 256
GRID_W = 64
GROUP_W = 256
N_HEADS = 4
A_DK = 32
A_DV = 64
B_DH = 64
C_DK = 32
C_DV = 64
GLA_RANK = 16
GLA_TAU = 16.0
D_HD = 64
WINDOW = 128
ROPE_THETA = 10000.0
EPS = 1e-6

LANES = 128
VMEM_LIMIT = 56 * 1024 * 1024

PA_W = 768
PB_W = 1152
PC_W = 640
PD_W = 512
PG_W = 1024
P_W = PA_W + PB_W + PC_W + PD_W + PG_W

ROW_TILE = 256
SCAN_CHUNK = 128
GLA_SUB = 16
ATT_TQ = 256
ATT_TK = 512
ATT_TILES_PER_STEP = 16
ATT_SUM_LIMIT = 2.0 ** 40

NEG_BIG = -1e30
LOG2E = 1.4426950408889634
A_VT = 80
N_CHAINS = 2 * N_HEADS
ATT_SKEW = 8


def _cparams(sem):
    return pltpu.CompilerParams(dimension_semantics=sem, vmem_limit_bytes=VMEM_LIMIT)


def _log_sigmoid(x):
    return jnp.minimum(x, 0.0) - jnp.log(1.0 + jnp.exp(-jnp.abs(x)))


def _sigmoid(x):
    return 1.0 / (1.0 + jnp.exp(-x))


def _lane_iota(shape):
    return lax.broadcasted_iota(jnp.int32, shape, len(shape) - 1)


def _head_mask(width, per_head, h, rows=1):
    lane = _lane_iota((rows, width))
    return (lane >= h * per_head) & (lane < (h + 1) * per_head)


def _group_matrix(n_rows, rows_per_group, n_cols, cols_per_group, value=1.0):
    r = lax.broadcasted_iota(jnp.int32, (n_rows, n_cols), 0)
    c = lax.broadcasted_iota(jnp.int32, (n_rows, n_cols), 1)
    same = jnp.zeros((n_rows, n_cols), jnp.bool_)
    for g in range(n_rows // rows_per_group):
        same = same | ((r >= g * rows_per_group) & (r < (g + 1) * rows_per_group)
                       & (c >= g * cols_per_group) & (c < (g + 1) * cols_per_group))
    return jnp.where(same, value, 0.0).astype(F32)


def _dot(a, b, precision=None):
    return jnp.dot(a, b, preferred_element_type=F32, precision=precision)


def _split3(x):
    hi = x.astype(BF16)
    r1 = x - hi.astype(F32)
    mid = r1.astype(BF16)
    lo = (r1 - mid.astype(F32)).astype(BF16)
    return hi, mid, lo


def _dot_f32_by_bf16(a, b_bf16, nt=False):
    f = _dot_nt if nt else _dot
    hi, mid, lo = _split3(a)
    return f(hi, b_bf16) + f(mid, b_bf16) + f(lo, b_bf16)


def _dot_bf16_by_f32(a_bf16, b):
    hi, mid, lo = _split3(b)
    return _dot(a_bf16, hi) + _dot(a_bf16, mid) + _dot(a_bf16, lo)


def _dot_nt(a, b, precision=None):
    return lax.dot_general(a, b, (((1,), (1,)), ((), ())), preferred_element_type=F32,
                           precision=precision)


def _mod_kernel(s_ref, w_ref, b_ref, o_ref):
    s = s_ref[...]
    s = s * _sigmoid(s)
    o_ref[0] = _dot(s, w_ref[0], HI) + b_ref[0]


def _modulation(s_in, w_mod, b_mod):
    depth, d, n = w_mod.shape
    tn = 1024
    return pl.pallas_call(
        _mod_kernel,
        grid=(depth, n // tn),
        in_specs=[pl.BlockSpec((8, d), lambda l, j: (0, 0)),
                  pl.BlockSpec((1, d, tn), lambda l, j: (l, 0, j)),
                  pl.BlockSpec((1, 1, tn), lambda l, j: (l, 0, j))],
        out_specs=pl.BlockSpec((1, 8, tn), lambda l, j: (l, 0, j)),
        out_shape=jax.ShapeDtypeStruct((depth, 8, n), F32),
        compiler_params=_cparams(("arbitrary", "arbitrary")),
        name="modulation",
    )(s_in, w_mod, b_mod.reshape(depth, 1, n))


def _stream_specs(stream, tm, first=0):
    nct = CTX // tm
    if isinstance(stream, tuple):
        ctx, lat = stream
        off = nct
    else:
        ctx = lat = stream
        off = 0
    d = lat.shape[-1]
    return ([pl.BlockSpec((1, tm, d), lambda b, i: (b, jnp.minimum(i + first, nct - 1), 0)),
             pl.BlockSpec((1, tm, d), lambda b, i: (b, jnp.maximum(i + first - off, 0), 0))], [ctx, lat])


def _stream_tile(ctx_ref, lat_ref, tile):
    return jnp.where(tile < CTX // ctx_ref.shape[1], ctx_ref[0], lat_ref[0])


def _proj_kernel(xc_ref, xl_ref, mod_ref, g_ref, w_ref, *refs):
    tab_refs, (pb_ref, pc_ref, pg_ref), att_refs = refs[0:9], refs[9:12], refs[12:18]
    x = _stream_tile(xc_ref, xl_ref, pl.program_id(1))
    ms = jnp.mean(x * x, axis=-1, keepdims=True)
    y = x * lax.rsqrt(ms + EPS) * g_ref[...]
    mod = mod_ref[0, 0]
    hx = y * (1.0 + mod[1:2]) + mod[0:1]
    hb = hx.astype(BF16)
    n_ad = PA_W + PD_W
    p_ad = _dot(hb, w_ref[:, 0:n_ad])
    _prep_attn(p_ad[:, 0:PA_W], p_ad[:, PA_W:n_ad], *tab_refs, *att_refs)
    p = _dot(hb, w_ref[:, n_ad:P_W])
    pb_ref[0] = p[:, 0:PB_W]
    pc_ref[0] = p[:, PB_W:PB_W + PC_W]
    pg_ref[0] = p[:, PB_W + PC_W:PB_W + PC_W + PG_W]


def _in_projection(stream, msel, norm_g, w_in_p, tabs, gains):
    bsz, d = msel.shape[0], msel.shape[-1]
    r = sum(a.shape[1] for a in stream) if isinstance(stream, tuple) else stream.shape[1]
    tm = ROW_TILE
    nct = CTX // tm
    row = lambda w: pl.BlockSpec((1, tm, w), lambda b, i: (b, i, 0))
    col = lambda h: pl.BlockSpec((1, h, tm), lambda b, i: (b, 0, i))
    vt = lambda n: pl.BlockSpec((1, n, A_VT, tm), lambda b, i: (b, 0, 0, i))
    rtab = pl.BlockSpec((1, 8, 256), lambda b, i: (i, 0, 0))
    ctab = pl.BlockSpec((GRID_W, 256), lambda b, i: (0, 0))
    stream_specs, stream_args = _stream_specs(stream, tm)
    return pl.pallas_call(
        _proj_kernel,
        grid=(bsz, r // tm),
        in_specs=[*stream_specs,
                  pl.BlockSpec((1, 1, 3, d), lambda b, i: (b, jnp.where(i >= nct, 1, 0), 0, 0)),
                  pl.BlockSpec((1, d), lambda b, i: (0, 0)),
                  pl.BlockSpec((d, P_W), lambda b, i: (0, 0)),
                  rtab, rtab, ctab, ctab, rtab, rtab, ctab, ctab,
                  pl.BlockSpec((8, 256), lambda b, i: (0, 0))],
        out_specs=[row(PB_W), row(PC_W), row(PG_W),
                   col(256), row(256), vt(N_HEADS), col(256), row(128), vt(2)],
        out_shape=[jax.ShapeDtypeStruct((bsz, r, PB_W), F32),
                   jax.ShapeDtypeStruct((bsz, r, PC_W), F32),
                   jax.ShapeDtypeStruct((bsz, r, PG_W), F32),
                   jax.ShapeDtypeStruct((bsz, 256, r), BF16),
                   jax.ShapeDtypeStruct((bsz, r, 256), BF16),
                   jax.ShapeDtypeStruct((bsz, N_HEADS, A_VT, r), BF16),
                   jax.ShapeDtypeStruct((bsz, 256, r), BF16),
                   jax.ShapeDtypeStruct((bsz, r, 128), BF16),
                   jax.ShapeDtypeStruct((bsz, 2, A_VT, r), BF16)],
        compiler_params=_cparams(("arbitrary", "arbitrary")),
        name="in_projection",
    )(*stream_args, msel, norm_g.reshape(1, d), w_in_p, *tabs, gains)


def _norm_rope(x, gain, cos, sin, group, half_pair, scale):
    w = x.shape[-1]
    gmat = _group_matrix(w, group, w, group, 1.0 / group).astype(BF16)
    ms = _dot_f32_by_bf16(x * x, gmat)
    xn = x * lax.rsqrt(ms + EPS) * gain
    lane = _lane_iota(xn.shape)
    first = (lane & (2 * half_pair - 1)) < half_pair
    partner = jnp.where(first, pltpu.roll(xn, w - half_pair, 1), pltpu.roll(xn, half_pair, 1))
    out = xn * cos + partner * sin
    return out * scale if scale != 1.0 else out


def _expand_rope(row_ref, col_ref, use_col):
    per_tile = ROW_TILE // GRID_W
    rt = row_ref[0]
    by_row = jnp.concatenate([jnp.broadcast_to(rt[j:j + 1], (GRID_W, 256)) for j in range(per_tile)], axis=0)
    by_col = jnp.concatenate([col_ref[...]] * per_tile, axis=0)
    return jnp.where(use_col, by_col, by_row)


def _prep_attn(pa, pd, rcosa_ref, rsina_ref, ccosa_ref, csina_ref,
               rcosd_ref, rsind_ref, ccosd_ref, csind_ref, gains_ref,
               qta_ref, ka_ref, vta_ref, qdt_ref, kd_ref, vdt_ref):
    latent = pl.program_id(1) >= CTX // ROW_TILE
    lane = _lane_iota((ROW_TILE, 256))
    col_a = ((lane & (A_DK // 2)) != 0) & latent
    col_d = ((lane & (D_HD // 2)) != 0) & latent
    cosa, sina = _expand_rope(rcosa_ref, ccosa_ref, col_a), _expand_rope(rsina_ref, csina_ref, col_a)
    gains = gains_ref[...]
    qa = _norm_rope(pa[:, 0:256], gains[0:1], cosa, sina, A_DK, A_DK // 4, A_DK ** -0.5 * LOG2E)
    ka = _norm_rope(pa[:, 256:512], gains[1:2], cosa, sina, A_DK, A_DK // 4, 1.0)
    qta_ref[0] = qa.T.astype(BF16)
    ka_ref[0] = ka.astype(BF16)
    vat = pa[:, 512:768].T
    rows = vat.shape[1]
    ones_rows = jnp.where(lax.broadcasted_iota(jnp.int32, (A_VT - A_DV, rows), 0) == 0, 1.0, 0.0).astype(BF16)
    for h in range(N_HEADS):
        vta_ref[0, h, 0:A_DV, :] = vat[h * A_DV:(h + 1) * A_DV].astype(BF16)
        vta_ref[0, h, A_DV:A_VT, :] = ones_rows
    cosd, sind = _expand_rope(rcosd_ref, ccosd_ref, col_d), _expand_rope(rsind_ref, csind_ref, col_d)
    qd = _norm_rope(pd[:, 0:256], gains[2:3], cosd, sind, D_HD, D_HD // 4, D_HD ** -0.5 * LOG2E)
    kd = _norm_rope(pd[:, 256:384], gains[3:4, 0:128], cosd[:, 0:128], sind[:, 0:128], D_HD, D_HD // 4, 1.0)
    qdt_ref[0] = qd.T.astype(BF16)
    kd_ref[0] = kd.astype(BF16)
    vdt = pd[:, 384:512].T
    for kv in range(2):
        vdt_ref[0, kv, 0:D_HD, :] = vdt[kv * D_HD:(kv + 1) * D_HD].astype(BF16)
        vdt_ref[0, kv, D_HD:A_VT, :] = ones_rows


def _diff_attn_kernel(qt_ref, k_ref, vt_ref, lam_ref, g_ref, o_ref, *scr, first_block, lam_init):
    wq_scr, acc_scr, m_scr = scr[0:N_CHAINS], scr[N_CHAINS:2 * N_CHAINS], scr[2 * N_CHAINS:3 * N_CHAINS]
    i = pl.program_id(1) + first_block
    tq = qt_ref.shape[2]
    n_tiles = (k_ref.shape[1] - CTX) // ATT_TK
    tiles_per_step = min(ATT_TILES_PER_STEP, n_tiles)
    step_keys = ATT_TK * tiles_per_step
    n_steps = jnp.where(i == 0, 0, n_tiles // tiles_per_step)
    row = lax.broadcasted_iota(jnp.int32, (LANES, tq), 0)
    for ch in range(N_CHAINS):
        grp, r_in = ch // 4, (ch % 4) * A_DK
        qt = qt_ref[0, grp * LANES:(grp + 1) * LANES, :]
        wq_scr[ch][...] = jnp.where((row >= r_in) & (row < r_in + A_DK), qt, jnp.zeros_like(qt))
        m_scr[ch][...] = jnp.full((8, tq), NEG_BIG, F32)
        acc_scr[ch][...] = jnp.zeros((A_VT, tq), F32)

    def process(tiles):
        units = [(ch, off, size) for off, size in tiles for ch in range(N_CHAINS)]

        def scores(u):
            ch, off, size = units[u]
            grp = ch // 4
            kt = k_ref[0, pl.ds(off, size), grp * LANES:(grp + 1) * LANES]
            return _dot(kt, wq_scr[ch][...])

        s = {u: scores(u) for u in range(min(ATT_SKEW, len(units)))}
        for u, (ch, off, size) in enumerate(units):
            m_old = m_scr[ch][0:1, :]
            m_new = jnp.maximum(m_old, jnp.max(s[u], axis=0, keepdims=True))
            alpha = jnp.exp2(m_old - m_new)
            p = jnp.exp2(s.pop(u) - m_new).astype(BF16)
            if u + ATT_SKEW < len(units):
                s[u + ATT_SKEW] = scores(u + ATT_SKEW)
            pv = _dot(vt_ref[0, ch // 2, :, pl.ds(off, size)], p)
            acc_scr[ch][...] = alpha * acc_scr[ch][...] + pv
            m_scr[ch][...] = jnp.broadcast_to(m_new, (8, tq))

    def process_fixed_max(tiles):
        units = [(ch, off, size) for off, size in tiles for ch in range(N_CHAINS)]

        def scores(u):
            ch, off, size = units[u]
            grp = ch // 4
            kt = k_ref[0, pl.ds(off, size), grp * LANES:(grp + 1) * LANES]
            return _dot(kt, wq_scr[ch][...])

        s = {u: scores(u) for u in range(min(ATT_SKEW, len(units)))}
        pv = [None] * N_CHAINS
        for u, (ch, off, size) in enumerate(units):
            p = jnp.exp2(s.pop(u) - m_scr[ch][0:1, :]).astype(BF16)
            if u + ATT_SKEW < len(units):
                s[u + ATT_SKEW] = scores(u + ATT_SKEW)
            d = _dot(vt_ref[0, ch // 2, :, pl.ds(off, size)], p)
            pv[ch] = d if pv[ch] is None else pv[ch] + d
        worst = None
        for ch in range(N_CHAINS):
            row_sum = acc_scr[ch][A_DV:A_DV + 1, :] + pv[ch][A_DV:A_DV + 1, :]
            worst = row_sum if worst is None else jnp.maximum(worst, row_sum)
        ok = jnp.max(worst) < ATT_SUM_LIMIT

        def commit():
            for ch in range(N_CHAINS):
                acc_scr[ch][...] = acc_scr[ch][...] + pv[ch]

        lax.cond(ok, commit, lambda: process_tiles(tiles[0][0], len(tiles)))

    def tile_offset(first, t):
        return pl.multiple_of(first + t * ATT_TK, math.gcd(CTX, ATT_TK))

    def process_tiles(first, n):
        def one(t, carry):
            process([(tile_offset(first, t), ATT_TK)])
            return carry
        lax.fori_loop(0, n, one, 0)

    process([(0, CTX)])

    def step(j, carry):
        process_fixed_max([(tile_offset(CTX + j * step_keys, t), ATT_TK) for t in range(tiles_per_step)])
        return carry

    lax.fori_loop(0, n_steps, step, 0)
    process_tiles(CTX + n_steps * step_keys, jnp.where(i == 0, 0, n_tiles % tiles_per_step))

    lp = lam_ref[...]
    lam = (jnp.exp(jnp.sum(lp[0:1] * lp[1:2], axis=-1, keepdims=True))
           - jnp.exp(jnp.sum(lp[2:3] * lp[3:4], axis=-1, keepdims=True)) + lam_init)
    outs = []
    for h in range(N_HEADS):
        comp = []
        for c in range(2):
            acc = acc_scr[2 * h + c][...]
            comp.append(acc[0:A_DV] / acc[A_DV:A_DV + 1])
        o = comp[0] - lam * comp[1]
        ms = jnp.mean(o * o, axis=0, keepdims=True)
        outs.append(o * lax.rsqrt(ms + EPS) * g_ref[...] * (1.0 - lam_init))
    o_ref[0] = jnp.concatenate(outs, axis=0).T.astype(o_ref.dtype)


def _diff_attention(qta, ka, vta, a_lam, a_subln, lam_init, with_ctx):
    bsz, r, _ = ka.shape
    tq = ATT_TQ
    first = 0 if with_ctx else CTX // tq
    nq = r // tq - first
    return pl.pallas_call(
        partial(_diff_attn_kernel, first_block=first, lam_init=lam_init),
        grid=(bsz, nq),
        in_specs=[pl.BlockSpec((1, 256, tq), lambda b, i: (b, 0, i + first)),
                  pl.BlockSpec((1, r, 256), lambda b, i: (b, 0, 0)),
                  pl.BlockSpec((1, N_HEADS, A_VT, r), lambda b, i: (b, 0, 0, 0)),
                  pl.BlockSpec((4, A_DK), lambda b, i: (0, 0)),
                  pl.BlockSpec((A_DV, 1), lambda b, i: (0, 0))],
        out_specs=pl.BlockSpec((1, tq, 256), lambda b, i: (b, i, 0)),
        out_shape=jax.ShapeDtypeStruct((bsz, nq * tq, 256), BF16),
        scratch_shapes=([pltpu.VMEM((LANES, tq), BF16)] * N_CHAINS
                        + [pltpu.VMEM((A_VT, tq), F32)] * N_CHAINS
                        + [pltpu.VMEM((8, tq), F32)] * N_CHAINS),
        compiler_params=_cparams(("arbitrary", "arbitrary")),
        name="diff_attention",
    )(qta, ka, vta, a_lam, a_subln.reshape(A_DV, 1))


def _window_attn_kernel(q_ref, k_ref, vt_ref, sink_ref, o_ref, *, first_block):
    i = pl.program_id(1) + first_block
    tq = q_ref.shape[2]
    r = k_ref.shape[1]
    span = tq + 2 * WINDOW
    start = jnp.clip(i * tq - WINDOW, CTX, r - span)
    start = pl.multiple_of(start, WINDOW)
    kwin = k_ref[0, pl.ds(start, span), :]
    kctx = k_ref[0, 0:CTX, :]
    kpos = start + lax.broadcasted_iota(jnp.int32, (span, tq), 0)
    qpos = i * tq + lax.broadcasted_iota(jnp.int32, (span, tq), 1)
    valid = (jnp.abs(kpos - qpos) <= WINDOW) & (qpos >= CTX)
    sink = sink_ref[...] * LOG2E
    zeros = jnp.zeros((D_HD, tq), BF16)
    scores = []
    for h in range(N_HEADS):
        qh = q_ref[0, h * D_HD:(h + 1) * D_HD, :]
        wq = jnp.concatenate([qh, zeros] if h // 2 == 0 else [zeros, qh], axis=0)
        scores.append((_dot(kwin, wq), _dot(kctx, wq)))
    outs = []
    for h in range(N_HEADS):
        kv = h // 2
        s_loc = jnp.where(valid, scores[h][0], -jnp.inf)
        s_ctx = scores[h][1]
        sk = sink[0:1, h:h + 1]
        m = jnp.maximum(jnp.maximum(jnp.max(s_loc, axis=0, keepdims=True),
                                    jnp.max(s_ctx, axis=0, keepdims=True)), sk)
        p_loc = jnp.exp2(s_loc - m).astype(BF16)
        p_ctx = jnp.exp2(s_ctx - m).astype(BF16)
        acc = (_dot(vt_ref[0, kv, :, pl.ds(start, span)], p_loc)
               + _dot(vt_ref[0, kv, :, 0:CTX], p_ctx))
        outs.append(acc[0:D_HD] / (acc[D_HD:D_HD + 1] + jnp.exp2(sk - m)))
    o_ref[0] = jnp.concatenate(outs, axis=0).T.astype(o_ref.dtype)


def _window_attention(qdt, kd, vdt, sink, with_ctx):
    bsz, r, _ = kd.shape
    tq = ROW_TILE
    first = 0 if with_ctx else CTX // tq
    return pl.pallas_call(
        partial(_window_attn_kernel, first_block=first),
        grid=(bsz, r // tq - first),
        in_specs=[pl.BlockSpec((1, 256, tq), lambda b, i: (b, 0, i + first)),
                  pl.BlockSpec((1, r, 128), lambda b, i: (b, 0, 0)),
                  pl.BlockSpec((1, 2, A_VT, r), lambda b, i: (b, 0, 0, 0)),
                  pl.BlockSpec((1, LANES), lambda b, i: (0, 0))],
        out_specs=pl.BlockSpec((1, tq, 256), lambda b, i: (b, i, 0)),
        out_shape=jax.ShapeDtypeStruct((bsz, r - first * tq, 256), BF16),
        compiler_params=_cparams(("arbitrary", "arbitrary")),
        name="window_attention",
    )(qdt, kd, vdt, sink)


def _chunk_of_step(i, n_chunks, n_ctx_chunks, rev):
    if not rev:
        return i
    return jnp.where(i < n_ctx_chunks, n_ctx_chunks - 1 - i, n_chunks - 1 - (i - n_ctx_chunks))


def _tri(n, rev):
    t = lax.broadcasted_iota(jnp.int32, (n, n), 0)
    s = lax.broadcasted_iota(jnp.int32, (n, n), 1)
    return jnp.where((s >= t) if rev else (s <= t), 1.0, 0.0).astype(BF16)


def _round_robin(gens):
    live = list(gens)
    while live:
        still = []
        for g in live:
            try:
                next(g)
                still.append(g)
            except StopIteration:
                pass
        live = still


def _scan_call(chain_fn, p, consts, halo_w, out_w, state_shapes, name):
    bsz, r, pw = p.shape
    ln = SCAN_CHUNK
    nc, ncc = r // ln, CTX // ln
    per8 = ln // 8
    dirs = (False, True)

    def kern(*refs):
        n_halo = 4 if halo_w else 0
        p_refs, halo_refs = refs[0:2], refs[2:2 + n_halo]
        const_refs = refs[2 + n_halo:2 + n_halo + len(consts)]
        o_refs = refs[2 + n_halo + len(consts):4 + n_halo + len(consts)]
        scr = refs[4 + n_halo + len(consts):]
        i = pl.program_id(0)

        @pl.when(i == 0)
        def _():
            for s_ref in scr:
                s_ref[...] = jnp.zeros_like(s_ref)

        gens = []
        for b in range(bsz):
            for d, rev in enumerate(dirs):
                c = (b * 2 + d) * len(state_shapes)
                halos = (halo_refs[2 * d], halo_refs[2 * d + 1]) if halo_w else ()
                gens.append(chain_fn(p_refs[d], *halos, *const_refs, o_refs[d], *scr[c:c + len(state_shapes)],
                                     b=b, rev=rev, chunk=_chunk_of_step(i, nc, ncc, rev),
                                     n_chunks=nc, n_ctx_chunks=ncc))
        _round_robin(gens)

    chunk = lambda rev: (lambda i: _chunk_of_step(i, nc, ncc, rev))
    in_specs = [pl.BlockSpec((bsz, ln, pw), lambda i, f=chunk(rev): (0, f(i), 0)) for rev in dirs]
    args = [p, p]
    if halo_w:
        for rev in dirs:
            f = chunk(rev)
            in_specs.append(pl.BlockSpec((bsz, 8, halo_w), lambda i, f=f: (0, jnp.maximum(f(i) * per8 - 1, 0), 0)))
            in_specs.append(pl.BlockSpec((bsz, 8, halo_w),
                                         lambda i, f=f: (0, jnp.minimum((f(i) + 1) * per8, r // 8 - 1), 0)))
            args += [p, p]
    for cst in consts:
        in_specs.append(pl.BlockSpec(cst.shape, lambda i, n=cst.ndim: (0,) * n))
        args.append(cst)
    return pl.pallas_call(
        kern,
        grid=(nc,),
        in_specs=in_specs,
        out_specs=[pl.BlockSpec((bsz, ln, out_w), lambda i, f=chunk(rev): (0, f(i), 0)) for rev in dirs],
        out_shape=[jax.ShapeDtypeStruct((bsz, r, out_w), F32)] * 2,
        scratch_shapes=[pltpu.VMEM(shp, F32) for _ in range(2 * bsz) for shp in state_shapes],
        compiler_params=_cparams(("arbitrary",)),
        name=name,
    )(*args)


def _mlstm_chain(p_ref, prev_ref, next_ref, cw_ref, cb_ref, gb_ref, o_ref, ct_scr, n_scr, m_scr,
                 *, b, rev, chunk, n_chunks, n_ctx_chunks):
    ln = p_ref.shape[1]
    hw = N_HEADS * B_DH

    x = p_ref[b, :, 0:2 * hw]
    seg_first = (chunk == 0) | (chunk == n_ctx_chunks)
    seg_last = (chunk == n_ctx_chunks - 1) | (chunk == n_chunks - 1)
    row_prev = jnp.where(seg_first, 0.0, prev_ref[b, 7:8, :])
    row_next = jnp.where(seg_last, 0.0, next_ref[b, 0:1, :])
    ridx = lax.broadcasted_iota(jnp.int32, x.shape, 0)
    x_prev = jnp.where(ridx == 0, row_prev, pltpu.roll(x, 1, 0))
    x_next = jnp.where(ridx == ln - 1, row_next, pltpu.roll(x, ln - 1, 0))
    cw = cw_ref[...]
    y = cw[0:1] * x_prev + cw[1:2] * x + cw[2:3] * x_next + cb_ref[...]
    y = y * _sigmoid(y)
    q = y[:, 0:hw]
    k = y[:, hw:2 * hw] * (B_DH ** -0.5)
    v = p_ref[b, :, 2 * hw:3 * hw]

    g = p_ref[b, :, 4 * hw:4 * hw + LANES] + gb_ref[...]
    base = 8 if rev else 0
    end = 0 if rev else ln - 1
    tri = _tri(ln, rev)
    gt8 = g.T[base:base + 8]
    lf_r = _log_sigmoid(gt8)
    b_r = _dot_f32_by_bf16(lf_r, tri, nt=True)
    lf_c = pltpu.roll(_log_sigmoid(g), LANES - N_HEADS, 1)
    c_col = g - _dot_bf16_by_f32(tri, lf_c)
    s_idx = lax.broadcasted_iota(jnp.int32, (ln, ln), 0)
    t_idx = lax.broadcasted_iota(jnp.int32, (ln, ln), 1)
    causal = (s_idx >= t_idx) if rev else (s_idx <= t_idx)

    qb = q.astype(BF16)
    kb = k.astype(BF16)
    vb = v.astype(BF16)
    eye = jnp.where(lax.broadcasted_iota(jnp.int32, (hw, hw), 0) == lax.broadcasted_iota(jnp.int32, (hw, hw), 1),
                    1.0, 0.0).astype(BF16)
    qt = _dot_nt(eye, qb).astype(BF16)
    vt = _dot_nt(eye, vb)
    ct_old = ct_scr[...]
    n_old = n_scr[...]
    m_old = m_scr[0:1, :]
    nq_inter = _dot_f32_by_bf16(n_old, qt)
    rows_hw = lax.broadcasted_iota(jnp.int32, (hw, ln), 0)
    qtm = [jnp.where((rows_hw >= h * B_DH) & (rows_hw < (h + 1) * B_DH), qt, jnp.zeros_like(qt))
           for h in range(N_HEADS)]
    yield
    st = [_dot(kb, qtm[h]) for h in range(N_HEADS)]
    inter = [_dot(ct_old[h * B_DH:(h + 1) * B_DH].astype(BF16), qtm[h]) for h in range(N_HEADS)]
    yield

    h_parts, w_end_rows, decay_cols, decay8, w_end8 = [], [], [], [], []
    m_new_row = jnp.zeros((1, LANES), F32)
    lane128 = _lane_iota((1, LANES))
    for h in range(N_HEADS):
        col = c_col[:, base + h:base + h + 1]
        br_h = b_r[4 + h:5 + h, :]
        li_r = gt8[h:h + 1, :]
        m_h = m_old[:, h:h + 1]
        d = jnp.where(causal, col + br_h, -jnp.inf)
        m_inter = br_h + m_h
        m_t = jnp.maximum(m_inter, jnp.max(d, axis=0, keepdims=True))
        g_inter = jnp.exp(m_inter - m_t)
        s = st[h] * jnp.exp(d - m_t)
        num = _dot(vt[h * B_DH:(h + 1) * B_DH].astype(BF16), s.astype(BF16)) + g_inter * inter[h]
        yield
        nq = jnp.sum(s, axis=0, keepdims=True) + g_inter * nq_inter[h:h + 1]
        den = jnp.maximum(jnp.abs(nq), jnp.exp(-m_t))
        h_parts.append(num / den)
        b_end = br_h[:, end:end + 1]
        g_r = b_end - br_h + li_r
        m_new = jnp.maximum(b_end + m_h, jnp.max(g_r, axis=-1, keepdims=True))
        w_end = jnp.exp(g_r - m_new)
        decay = jnp.exp(b_end + m_h - m_new)
        w_end_rows.append(jnp.broadcast_to(w_end, (B_DH, ln)))
        decay_cols.append(jnp.broadcast_to(decay, (B_DH, 1)))
        w_end8.append(w_end)
        decay8.append(decay)
        m_new_row = jnp.where(lane128 == h, m_new, m_new_row)
    o_ref[b] = jnp.concatenate(h_parts, axis=0).T

    vtw = (vt * jnp.concatenate(w_end_rows, axis=0)).astype(BF16)
    ct_scr[...] = jnp.concatenate(decay_cols, axis=0) * ct_old + _dot(vtw, kb)
    pad4 = [jnp.zeros((1, ln), F32)] * (8 - N_HEADS)
    n_upd = _dot_f32_by_bf16(jnp.concatenate(w_end8 + pad4, axis=0), kb)
    own = _group_matrix(8, 1, hw, B_DH)
    n_scr[...] = (jnp.concatenate(decay8 + [jnp.zeros((1, 1), F32)] * (8 - N_HEADS), axis=0) * n_old
                  + own * n_upd)
    m_scr[0:1, :] = m_new_row


def _mlstm(pb, conv_w, conv_b, gate_b):
    hw = N_HEADS * B_DH
    return _scan_call(_mlstm_chain, pb, (conv_w, conv_b, gate_b), 2 * hw, hw,
                      ((hw, hw), (8, hw), (8, LANES)), "mlstm")


def _gla_chain(p_ref, wg_ref, bg_ref, o_ref, s_scr, *, b, rev, chunk, n_chunks, n_ctx_chunks):
    ln = p_ref.shape[1]
    kw_ = N_HEADS * C_DK
    vw_ = N_HEADS * C_DV
    c = GLA_SUB
    nb = ln // c
    z = 1 if rev else 0

    q = p_ref[b, :, 0:kw_] * (C_DK ** -0.5)
    k = p_ref[b, :, kw_:2 * kw_]
    v = p_ref[b, :, 2 * kw_:2 * kw_ + vw_]
    lr = p_ref[b, :, 2 * kw_ + vw_:2 * kw_ + vw_ + LANES]
    la = _log_sigmoid(_dot(lr, wg_ref[z], HI) + bg_ref[z]) * (1.0 / GLA_TAU)
    bc = _dot_bf16_by_f32(_tri(ln, rev), la)
    vb = v.astype(BF16)
    blk = _group_matrix(kw_, C_DK, vw_, C_DV)
    blk_b = blk.astype(BF16)

    s_old = s_scr[...]
    o_acc = _dot((q * jnp.exp(bc)).astype(BF16), s_old.astype(BF16))
    yield

    t_loc = lax.broadcasted_iota(jnp.int32, (c, kw_), 0)
    off_order = [jnp.where((t_loc >= t) if rev else (t_loc <= t), 0.0, NEG_BIG) for t in range(c)]
    pair_row = lax.broadcasted_iota(jnp.int32, (c, c * c), 0)
    pair_col = lax.broadcasted_iota(jnp.int32, (c, c * c), 1)
    sum_s = jnp.where((pair_col >= pair_row * c) & (pair_col < pair_row * c + c), 1.0, 0.0).astype(BF16)
    pair_sums = []
    for sub in range(nb):
        sl = slice(sub * c, (sub + 1) * c)
        bcb, qb_, kb_ = bc[sl], q[sl], k[sl]
        pieces = [(qb_[t:t + 1] * kb_) * jnp.exp(bcb[t:t + 1] - bcb + off_order[t]) for t in range(c)]
        e = jnp.concatenate(pieces, axis=0)
        pair_sums.append(_dot(e.astype(BF16), blk_b))
        if sub % 4 == 3:
            yield
    diag_parts = []
    for sub in range(nb):
        prod = (pair_sums[sub] * jnp.concatenate([v[sub * c:(sub + 1) * c]] * c, axis=0)).astype(BF16)
        diag_parts.append(_dot(sum_s, prod))
    yield
    o_acc = o_acc + jnp.concatenate(diag_parts, axis=0)

    rmask = lax.broadcasted_iota(jnp.int32, (N_HEADS * c, kw_), 0)
    lmask = _lane_iota((N_HEADS * c, kw_))
    rmask_v = lax.broadcasted_iota(jnp.int32, (N_HEADS * c, vw_), 0)
    lmask_v = _lane_iota((N_HEADS * c, vw_))
    same_k = jnp.zeros((N_HEADS * c, kw_), jnp.bool_)
    same_v = jnp.zeros((N_HEADS * c, vw_), jnp.bool_)
    for h in range(N_HEADS):
        same_k = same_k | ((rmask >= h * c) & (rmask < (h + 1) * c) & (lmask >= h * C_DK) & (lmask < (h + 1) * C_DK))
        same_v = same_v | ((rmask_v >= h * c) & (rmask_v < (h + 1) * c) & (lmask_v >= h * C_DV) & (lmask_v < (h + 1) * C_DV))
    off_full = jnp.zeros((ln, vw_), F32)
    for j in range(nb):
        if rev:
            if j == 0:
                continue
            rows = slice(0, j * c)
            edge = bc[j * c:j * c + 1]
        else:
            if j == nb - 1:
                continue
            rows = slice((j + 1) * c, ln)
            edge = bc[(j + 1) * c - 1:(j + 1) * c]
        sl = slice(j * c, (j + 1) * c)
        qs = q[rows] * jnp.exp(bc[rows] - edge)
        ks = k[sl] * jnp.exp(edge - bc[sl])
        kbd = jnp.where(same_k, jnp.concatenate([ks] * N_HEADS, axis=0), 0.0)
        vbd = jnp.where(same_v, jnp.concatenate([v[sl]] * N_HEADS, axis=0), 0.0)
        a = _dot_nt(qs.astype(BF16), kbd.astype(BF16))
        contrib = _dot(a.astype(BF16), vbd.astype(BF16))
        n_rows = contrib.shape[0]
        pad = jnp.zeros((ln - n_rows, vw_), F32)
        off_full = off_full + (jnp.concatenate([contrib, pad], axis=0) if rev
                               else jnp.concatenate([pad, contrib], axis=0))
        yield
    o_ref[b] = o_acc + off_full

    end = 0 if rev else ln - 1
    b_end = bc[end:end + 1]
    kd = k * jnp.exp(b_end - bc)
    upd = _dot(kd.T.astype(BF16), vb)
    eye = jnp.where(lax.broadcasted_iota(jnp.int32, (kw_, kw_), 0)
                    == lax.broadcasted_iota(jnp.int32, (kw_, kw_), 1), 1.0, 0.0).astype(F32)
    decay_col = jnp.sum(eye * jnp.exp(b_end), axis=-1, keepdims=True)
    s_scr[...] = decay_col * s_old + blk * upd


def _gla(pc, wg_pad, bg):
    return _scan_call(_gla_chain, pc, (wg_pad, bg), 0, N_HEADS * C_DV,
                      ((N_HEADS * C_DK, N_HEADS * C_DV),), "gla")


def _out_kernel(a_ref, bf_ref, bb_ref, bo_ref, cf_ref, cb_ref, d_ref, pg_ref, xc_ref, xl_ref, mod_ref,
                gn_ref, w_ref, o_ref, *, first_block):
    gmat = _group_matrix(GROUP_W, 64, GROUP_W, 64, 1.0 / 64).astype(BF16)
    gn = gn_ref[...]

    def rms64(hsum, gain):
        ms = _dot_f32_by_bf16(hsum * hsum, gmat)
        return hsum * lax.rsqrt(ms + EPS) * gain

    b_out = rms64(bf_ref[0] + bb_ref[0], gn[0:1]) * _sigmoid(bo_ref[0])
    c_out = rms64(cf_ref[0] + cb_ref[0], gn[1:2])
    y = jnp.concatenate([a_ref[0].astype(F32), b_out, c_out, d_ref[0].astype(F32)], axis=-1)
    gate = pg_ref[0]
    y = y * (gate * _sigmoid(gate))
    upd = _dot(y.astype(BF16), w_ref[...])
    x = _stream_tile(xc_ref, xl_ref, pl.program_id(1) + first_block)
    o_ref[0] = x + mod_ref[0, 0][2:3] * upd


def _out_projection(a_o, bf, bb, pb, cf, cb, d_o, pg, stream, msel, gains, w_out_b, with_ctx):
    bsz, r, d = pg.shape[0], pg.shape[1], msel.shape[-1]
    tm = ROW_TILE
    nct = CTX // tm
    first = 0 if with_ctx else nct
    row = lambda w: pl.BlockSpec((1, tm, w), lambda b, i: (b, i + first, 0))
    att = pl.BlockSpec((1, tm, 256), lambda b, i: (b, i, 0))
    out_rows = r - first * tm
    stream_specs, stream_args = _stream_specs(stream, tm, first)
    return pl.pallas_call(
        partial(_out_kernel, first_block=first),
        grid=(bsz, r // tm - first),
        in_specs=[att, row(256), row(256),
                  pl.BlockSpec((1, tm, 256), lambda b, i: (b, i + first, 3)),
                  row(256), row(256), att, row(PG_W), *stream_specs,
                  pl.BlockSpec((1, 1, 3, d), lambda b, i: (b, jnp.where(i + first >= nct, 1, 0), 0, 0)),
                  pl.BlockSpec((8, 256), lambda b, i: (0, 0)),
                  pl.BlockSpec((GROUP_W * 4, d), lambda b, i: (0, 0))],
        out_specs=pl.BlockSpec((1, tm, d), lambda b, i: (b, i, 0)),
        out_shape=jax.ShapeDtypeStruct((bsz, out_rows, d), F32),
        compiler_params=_cparams(("arbitrary", "arbitrary")),
        name="out_projection",
    )(a_o, bf, bb, pb, cf, cb, d_o, pg, *stream_args, msel, gains, w_out_b)


def _pad_cols(w, width):
    return jnp.pad(w, ((0, 0), (0, width - w.shape[1])))


def _layout_w_in(w):
    a = w[:, 0:768]
    bq, bk, bv = w[:, 768:1024], w[:, 1024:1280], w[:, 1280:1536]
    bg, bo = w[:, 1536:1552], w[:, 1552:1808]
    cq, ck, cv, clr = w[:, 1808:1936], w[:, 1936:2064], w[:, 2064:2320], w[:, 2320:2352]
    dd = w[:, 2352:2864]
    gg = w[:, 2864:3888]
    return jnp.concatenate([a, dd, bq, bk, bv, bo, _pad_cols(bg, LANES), cq, ck, cv, _pad_cols(clr, LANES), gg],
                           axis=1)


def _rope_tables(t, hd):
    nq = hd // 4
    per_tile = ROW_TILE // GRID_W
    inv = ROPE_THETA ** (-jnp.arange(nq, dtype=F32) / nq)
    lane = jnp.arange(256)
    within = (lane % hd) % (hd // 2)
    freq = inv[within % nq][None, :]
    sign = jnp.where(within < nq, -1.0, 1.0)[None, :]

    def tables(pos):
        ang = pos.astype(F32)[:, None] * freq
        return jnp.cos(ang), sign * jnp.sin(ang)

    row_cos, row_sin = tables(jnp.arange(t // GRID_W))
    col_cos, col_sin = tables(jnp.arange(GRID_W))

    def per_tile_blocks(tab, ctx_value):
        tab = jnp.pad(tab.reshape(-1, per_tile, 256), ((0, 0), (0, 8 - per_tile), (0, 0)))
        return jnp.concatenate([jnp.full((CTX // ROW_TILE, 8, 256), ctx_value, F32), tab], axis=0)

    return per_tile_blocks(row_cos, 1.0), per_tile_blocks(row_sin, 0.0), col_cos, col_sin


def _tile_lanes(g, width):
    return jnp.tile(g, width // g.shape[0])


def kernel(x, c, ctx, c_ctx, w_mod, b_mod, norm_g, w_in, w_out, a_qn, a_kn, a_lam, a_subln, b_conv_w, b_conv_b, b_gate_b, b_outn, c_wg, c_bg, c_outn, d_qn, d_kn, d_sink):
    bsz, t, d = x.shape
    depth = w_mod.shape[0]
    assert d == D_MODEL and ctx.shape[1] == CTX and t % ROW_TILE == 0 and bsz + 1 <= 8

    s_in = jnp.zeros((8, d), F32).at[:bsz].set(c).at[bsz].set(c_ctx)
    mod = _modulation(s_in, w_mod, b_mod).reshape(depth, 8, 3, d)
    tabs = _rope_tables(t, A_DK) + _rope_tables(t, D_HD)

    xc = (ctx, x)
    for l in range(depth):
        with_ctx = l < depth - 1
        lam_init = 0.8 - 0.6 * math.exp(-0.3 * l)
        msel = jnp.stack([jnp.broadcast_to(mod[l, bsz], (bsz, 3, d)), mod[l, :bsz]], axis=1)
        w_in_p = _layout_w_in(w_in[l]).astype(BF16)
        gains_att = jnp.zeros((8, 256), F32)
        gains_att = gains_att.at[0].set(_tile_lanes(a_qn[l], 256)).at[1].set(_tile_lanes(a_kn[l], 256))
        gains_att = gains_att.at[2].set(_tile_lanes(d_qn[l], 256)).at[3].set(_tile_lanes(d_kn[l], 256))
        pb, pc, pg, qta, ka, vta, qdt, kd, vdt = _in_projection(xc, msel, norm_g[l], w_in_p, tabs, gains_att)
        a_o = _diff_attention(qta, ka, vta, a_lam[l], a_subln[l], lam_init, with_ctx)
        sink = jnp.zeros((1, LANES), F32).at[0, :N_HEADS].set(d_sink[l])
        d_o = _window_attention(qdt, kd, vdt, sink, with_ctx)

        gate_b = jnp.zeros((1, LANES), F32).at[0, :16].set(b_gate_b[l].reshape(16))
        conv_b = b_conv_b[l].reshape(1, -1)
        bf, bb = _mlstm(pb, b_conv_w[l], conv_b, gate_b)

        wg_pad = jnp.zeros((2, LANES, LANES), F32)
        for z in range(2):
            wg_pad = wg_pad.at[z, z * GLA_RANK:(z + 1) * GLA_RANK].set(c_wg[l, z])
        cf, cb = _gla(pc, wg_pad, c_bg[l].reshape(2, 1, LANES))

        gains_out = jnp.zeros((8, 256), F32)
        gains_out = gains_out.at[0].set(_tile_lanes(b_outn[l], 256)).at[1].set(_tile_lanes(c_outn[l], 256))
        xc = _out_projection(a_o, bf, bb, pb, cf, cb, d_o, pg, xc, msel, gains_out,
                             w_out[l].astype(BF16), with_ctx)
    return xc
```

```python
import math
from functools import partial

import jax
import jax.numpy as jnp
from jax import lax
from jax.experimental import pallas as pl
from jax.experimental.pallas import tpu as pltpu

F32 = jnp.float32
BF16 = jnp.bfloat16
HI = lax.Precision.HIGHEST

D_MODEL = 1024
CTX = 256
GRID_W = 64
GROUP_W = 256
N_HEADS = 4
A_DK = 32
A_DV = 64
B_DH = 64
C_DK = 32
C_DV = 64
GLA_RANK = 16
GLA_TAU = 16.0
D_HD = 64
WINDOW = 128
ROPE_THETA = 10000.0
EPS = 1e-6

LANES = 128
VMEM_LIMIT = 56 * 1024 * 1024

PA_W = 768
PB_W = 1152
PC_W = 640
PD_W = 512
PG_W = 1024
P_W = PA_W + PB_W + PC_W + PD_W + PG_W

ROW_TILE = 256
SCAN_CHUNK = 128
GLA_SUB = 16
ATT_TQ = 256
ATT_TK = 512
ATT_TILES_PER_STEP = 16
ATT_SUM_LIMIT = 2.0 ** 40

NEG_BIG = -1e30
LOG2E = 1.4426950408889634
A_VT = 80
N_CHAINS = 2 * N_HEADS
ATT_SKEW = 8


def _cparams(sem):
    return pltpu.CompilerParams(dimension_semantics=sem, vmem_limit_bytes=VMEM_LIMIT)


def _log_sigmoid(x):
    return jnp.minimum(x, 0.0) - jnp.log(1.0 + jnp.exp(-jnp.abs(x)))


def _sigmoid(x):
    return 1.0 / (1.0 + jnp.exp(-x))


def _lane_iota(shape):
    return lax.broadcasted_iota(jnp.int32, shape, len(shape) - 1)


def _head_mask(width, per_head, h, rows=1):
    lane = _lane_iota((rows, width))
    return (lane >= h * per_head) & (lane < (h + 1) * per_head)


def _group_matrix(n_rows, rows_per_group, n_cols, cols_per_group, value=1.0):
    r = lax.broadcasted_iota(jnp.int32, (n_rows, n_cols), 0)
    c = lax.broadcasted_iota(jnp.int32, (n_rows, n_cols), 1)
    same = jnp.zeros((n_rows, n_cols), jnp.bool_)
    for g in range(n_rows // rows_per_group):
        same = same | ((r >= g * rows_per_group) & (r < (g + 1) * rows_per_group)
                       & (c >= g * cols_per_group) & (c < (g + 1) * cols_per_group))
    return jnp.where(same, value, 0.0).astype(F32)


def _dot(a, b, precision=None):
    return jnp.dot(a, b, preferred_element_type=F32, precision=precision)


def _dot_bf16x3(a, b):
    a_hi = a.astype(BF16)
    a_lo = (a - a_hi.astype(F32)).astype(BF16)
    b_hi = b.astype(BF16)
    b_lo = (b - b_hi.astype(F32)).astype(BF16)
    return _dot(a_hi, b_hi) + _dot(a_hi, b_lo) + _dot(a_lo, b_hi)


def _split3(x):
    hi = x.astype(BF16)
    r1 = x - hi.astype(F32)
    mid = r1.astype(BF16)
    lo = (r1 - mid.astype(F32)).astype(BF16)
    return hi, mid, lo


def _dot_f32_by_bf16(a, b_bf16, nt=False):
    f = _dot_nt if nt else _dot
    hi, mid, lo = _split3(a)
    return f(hi, b_bf16) + f(mid, b_bf16) + f(lo, b_bf16)


def _dot_bf16_by_f32(a_bf16, b):
    hi, mid, lo = _split3(b)
    return _dot(a_bf16, hi) + _dot(a_bf16, mid) + _dot(a_bf16, lo)


def _dot_nt(a, b, precision=None):
    return lax.dot_general(a, b, (((1,), (1,)), ((), ())), preferred_element_type=F32,
                           precision=precision)


def _mod_kernel(s_ref, w_ref, b_ref, o_ref):
    s = s_ref[...]
    s = s * _sigmoid(s)
    o_ref[0] = _dot(s, w_ref[0], HI) + b_ref[0]


def _modulation(s_in, w_mod, b_mod):
    depth, d, n = w_mod.shape
    tn = 1024
    return pl.pallas_call(
        _mod_kernel,
        grid=(depth, n // tn),
        in_specs=[pl.BlockSpec((8, d), lambda l, j: (0, 0)),
                  pl.BlockSpec((1, d, tn), lambda l, j: (l, 0, j)),
                  pl.BlockSpec((1, 1, tn), lambda l, j: (l, 0, j))],
        out_specs=pl.BlockSpec((1, 8, tn), lambda l, j: (l, 0, j)),
        out_shape=jax.ShapeDtypeStruct((depth, 8, n), F32),
        compiler_params=_cparams(("arbitrary", "arbitrary")),
        name="modulation",
    )(s_in, w_mod, b_mod.reshape(depth, 1, n))


def _stream_specs(stream, tm, first=0):
    nct = CTX // tm
    if isinstance(stream, tuple):
        ctx, lat = stream
        off = nct
    else:
        ctx = lat = stream
        off = 0
    d = lat.shape[-1]
    return ([pl.BlockSpec((1, tm, d), lambda b, i: (b, jnp.minimum(i + first, nct - 1), 0)),
             pl.BlockSpec((1, tm, d), lambda b, i: (b, jnp.maximum(i + first - off, 0), 0))], [ctx, lat])


def _stream_tile(ctx_ref, lat_ref, tile):
    return jnp.where(tile < CTX // ctx_ref.shape[1], ctx_ref[0], lat_ref[0])


def _proj_kernel(xc_ref, xl_ref, mod_ref, g_ref, w_ref, *refs):
    tab_refs, (pb_ref, pc_ref, pg_ref), att_refs = refs[0:9], refs[9:12], refs[12:18]
    x = _stream_tile(xc_ref, xl_ref, pl.program_id(1))
    ms = jnp.mean(x * x, axis=-1, keepdims=True)
    y = x * lax.rsqrt(ms + EPS) * g_ref[...]
    mod = mod_ref[0, 0]
    hx = y * (1.0 + mod[1:2]) + mod[0:1]
    hb = hx.astype(BF16)
    n_ad = PA_W + PD_W
    p_ad = _dot(hb, w_ref[:, 0:n_ad])
    _prep_attn(p_ad[:, 0:PA_W], p_ad[:, PA_W:n_ad], *tab_refs, *att_refs)
    p = _dot(hb, w_ref[:, n_ad:P_W])
    pb_ref[0] = p[:, 0:PB_W]
    pc_ref[0] = p[:, PB_W:PB_W + PC_W]
    pg_ref[0] = p[:, PB_W + PC_W:PB_W + PC_W + PG_W].astype(BF16)


def _in_projection(stream, msel, norm_g, w_in_p, tabs, gains):
    bsz, d = msel.shape[0], msel.shape[-1]
    r = sum(a.shape[1] for a in stream) if isinstance(stream, tuple) else stream.shape[1]
    tm = ROW_TILE
    nct = CTX // tm
    row = lambda w: pl.BlockSpec((1, tm, w), lambda b, i: (b, i, 0))
    col = lambda h: pl.BlockSpec((1, h, tm), lambda b, i: (b, 0, i))
    vt = lambda n, rows: pl.BlockSpec((1, n, rows, tm), lambda b, i: (b, 0, 0, i))
    rtab = pl.BlockSpec((1, 8, 256), lambda b, i: (i, 0, 0))
    ctab = pl.BlockSpec((GRID_W, 256), lambda b, i: (0, 0))
    stream_specs, stream_args = _stream_specs(stream, tm)
    return pl.pallas_call(
        _proj_kernel,
        grid=(bsz, r // tm),
        in_specs=[*stream_specs,
                  pl.BlockSpec((1, 1, 3, d), lambda b, i: (b, jnp.where(i >= nct, 1, 0), 0, 0)),
                  pl.BlockSpec((1, d), lambda b, i: (0, 0)),
                  pl.BlockSpec((d, P_W), lambda b, i: (0, 0)),
                  rtab, rtab, ctab, ctab, rtab, rtab, ctab, ctab,
                  pl.BlockSpec((8, 256), lambda b, i: (0, 0))],
        out_specs=[row(PB_W), row(PC_W), row(PG_W),
                   col(256), row(256), vt(N_HEADS, A_VT), col(256), row(128), vt(2, A_VT)],
        out_shape=[jax.ShapeDtypeStruct((bsz, r, PB_W), F32),
                   jax.ShapeDtypeStruct((bsz, r, PC_W), F32),
                   jax.ShapeDtypeStruct((bsz, r, PG_W), BF16),
                   jax.ShapeDtypeStruct((bsz, 256, r), BF16),
                   jax.ShapeDtypeStruct((bsz, r, 256), BF16),
                   jax.ShapeDtypeStruct((bsz, N_HEADS, A_VT, r), BF16),
                   jax.ShapeDtypeStruct((bsz, 256, r), BF16),
                   jax.ShapeDtypeStruct((bsz, r, 128), BF16),
                   jax.ShapeDtypeStruct((bsz, 2, A_VT, r), BF16)],
        compiler_params=_cparams(("arbitrary", "arbitrary")),
        name="in_projection",
    )(*stream_args, msel, norm_g.reshape(1, d), w_in_p, *tabs, gains)


def _norm_rope(x, gain, cos, sin, group, half_pair, scale):
    w = x.shape[-1]
    gmat = _group_matrix(w, group, w, group, 1.0 / group).astype(BF16)
    ms = _dot_f32_by_bf16(x * x, gmat)
    xn = x * lax.rsqrt(ms + EPS) * gain
    lane = _lane_iota(xn.shape)
    first = (lane & (2 * half_pair - 1)) < half_pair
    partner = jnp.where(first, pltpu.roll(xn, w - half_pair, 1), pltpu.roll(xn, half_pair, 1))
    out = xn * cos + partner * sin
    return out * scale if scale != 1.0 else out


def _expand_rope(row_ref, col_ref, use_col):
    per_tile = ROW_TILE // GRID_W
    rt = row_ref[0]
    by_row = jnp.concatenate([jnp.broadcast_to(rt[j:j + 1], (GRID_W, 256)) for j in range(per_tile)], axis=0)
    by_col = jnp.concatenate([col_ref[...]] * per_tile, axis=0)
    return jnp.where(use_col, by_col, by_row)


def _prep_attn(pa, pd, rcosa_ref, rsina_ref, ccosa_ref, csina_ref,
               rcosd_ref, rsind_ref, ccosd_ref, csind_ref, gains_ref,
               qta_ref, ka_ref, vta_ref, qdt_ref, kd_ref, vdt_ref):
    latent = pl.program_id(1) >= CTX // ROW_TILE
    lane = _lane_iota((ROW_TILE, 256))
    col_a = ((lane & (A_DK // 2)) != 0) & latent
    col_d = ((lane & (D_HD // 2)) != 0) & latent
    cosa, sina = _expand_rope(rcosa_ref, ccosa_ref, col_a), _expand_rope(rsina_ref, csina_ref, col_a)
    gains = gains_ref[...]
    qa = _norm_rope(pa[:, 0:256], gains[0:1], cosa, sina, A_DK, A_DK // 4, A_DK ** -0.5 * LOG2E)
    ka = _norm_rope(pa[:, 256:512], gains[1:2], cosa, sina, A_DK, A_DK // 4, 1.0)
    qta_ref[0] = qa.T.astype(BF16)
    ka_ref[0] = ka.astype(BF16)
    vat = pa[:, 512:768].T
    rows = vat.shape[1]
    ones_rows = jnp.where(lax.broadcasted_iota(jnp.int32, (A_VT - A_DV, rows), 0) == 0, 1.0, 0.0).astype(BF16)
    for h in range(N_HEADS):
        vta_ref[0, h, 0:A_DV, :] = vat[h * A_DV:(h + 1) * A_DV].astype(BF16)
        vta_ref[0, h, A_DV:A_VT, :] = ones_rows
    cosd, sind = _expand_rope(rcosd_ref, ccosd_ref, col_d), _expand_rope(rsind_ref, csind_ref, col_d)
    qd = _norm_rope(pd[:, 0:256], gains[2:3], cosd, sind, D_HD, D_HD // 4, D_HD ** -0.5 * LOG2E)
    kd = _norm_rope(pd[:, 256:384], gains[3:4, 0:128], cosd[:, 0:128], sind[:, 0:128], D_HD, D_HD // 4, 1.0)
    qdt_ref[0] = qd.T.astype(BF16)
    kd_ref[0] = kd.astype(BF16)
    vdt = pd[:, 384:512].T
    for kv in range(2):
        vdt_ref[0, kv, 0:D_HD, :] = vdt[kv * D_HD:(kv + 1) * D_HD].astype(BF16)
        vdt_ref[0, kv, D_HD:A_VT, :] = ones_rows


def _diff_attn_kernel(qt_ref, k_ref, vt_ref, lam_ref, g_ref, o_ref, *scr, first_block, lam_init):
    wq_scr, acc_scr, m_scr = scr[0:N_CHAINS], scr[N_CHAINS:2 * N_CHAINS], scr[2 * N_CHAINS:3 * N_CHAINS]
    i = pl.program_id(1) + first_block
    tq = qt_ref.shape[2]
    n_tiles = (k_ref.shape[1] - CTX) // ATT_TK
    tiles_per_step = min(ATT_TILES_PER_STEP, n_tiles)
    step_keys = ATT_TK * tiles_per_step
    n_steps = jnp.where(i == 0, 0, n_tiles // tiles_per_step)
    row = lax.broadcasted_iota(jnp.int32, (LANES, tq), 0)
    for ch in range(N_CHAINS):
        grp, r_in = ch // 4, (ch % 4) * A_DK
        qt = qt_ref[0, grp * LANES:(grp + 1) * LANES, :]
        wq_scr[ch][...] = jnp.where((row >= r_in) & (row < r_in + A_DK), qt, jnp.zeros_like(qt))
        m_scr[ch][...] = jnp.full((8, tq), NEG_BIG, F32)
        acc_scr[ch][...] = jnp.zeros((A_VT, tq), F32)

    def process(tiles):
        units = [(ch, off, size) for off, size in tiles for ch in range(N_CHAINS)]

        def scores(u):
            ch, off, size = units[u]
            grp = ch // 4
            kt = k_ref[0, pl.ds(off, size), grp * LANES:(grp + 1) * LANES]
            return _dot(kt, wq_scr[ch][...])

        s = {u: scores(u) for u in range(min(ATT_SKEW, len(units)))}
        for u, (ch, off, size) in enumerate(units):
            m_old = m_scr[ch][0:1, :]
            m_new = jnp.maximum(m_old, jnp.max(s[u], axis=0, keepdims=True))
            alpha = jnp.exp2(m_old - m_new)
            p = jnp.exp2(s.pop(u) - m_new).astype(BF16)
            if u + ATT_SKEW < len(units):
                s[u + ATT_SKEW] = scores(u + ATT_SKEW)
            pv = _dot(vt_ref[0, ch // 2, :, pl.ds(off, size)], p)
            acc_scr[ch][...] = alpha * acc_scr[ch][...] + pv
            m_scr[ch][...] = jnp.broadcast_to(m_new, (8, tq))

    def process_fixed_max(tiles):
        units = [(ch, off, size) for off, size in tiles for ch in range(N_CHAINS)]

        def scores(u):
            ch, off, size = units[u]
            grp = ch // 4
            kt = k_ref[0, pl.ds(off, size), grp * LANES:(grp + 1) * LANES]
            return _dot(kt, wq_scr[ch][...])

        s = {u: scores(u) for u in range(min(ATT_SKEW, len(units)))}
        pv = [None] * N_CHAINS
        for u, (ch, off, size) in enumerate(units):
            p = jnp.exp2(s.pop(u) - m_scr[ch][0:1, :]).astype(BF16)
            if u + ATT_SKEW < len(units):
                s[u + ATT_SKEW] = scores(u + ATT_SKEW)
            d = _dot(vt_ref[0, ch // 2, :, pl.ds(off, size)], p)
            pv[ch] = d if pv[ch] is None else pv[ch] + d
        worst = None
        for ch in range(N_CHAINS):
            row_sum = acc_scr[ch][A_DV:A_DV + 1, :] + pv[ch][A_DV:A_DV + 1, :]
            worst = row_sum if worst is None else jnp.maximum(worst, row_sum)
        ok = jnp.max(worst) < ATT_SUM_LIMIT

        def commit():
            for ch in range(N_CHAINS):
                acc_scr[ch][...] = acc_scr[ch][...] + pv[ch]

        lax.cond(ok, commit, lambda: process_tiles(tiles[0][0], len(tiles)))

    def tile_offset(first, t):
        return pl.multiple_of(first + t * ATT_TK, math.gcd(CTX, ATT_TK))

    def process_tiles(first, n):
        def one(t, carry):
            process([(tile_offset(first, t), ATT_TK)])
            return carry
        lax.fori_loop(0, n, one, 0)

    process([(0, CTX)])

    def step(j, carry):
        process_fixed_max([(tile_offset(CTX + j * step_keys, t), ATT_TK) for t in range(tiles_per_step)])
        return carry

    lax.fori_loop(0, n_steps, step, 0)
    process_tiles(CTX + n_steps * step_keys, jnp.where(i == 0, 0, n_tiles % tiles_per_step))

    lp = lam_ref[...]
    lam = (jnp.exp(jnp.sum(lp[0:1] * lp[1:2], axis=-1, keepdims=True))
           - jnp.exp(jnp.sum(lp[2:3] * lp[3:4], axis=-1, keepdims=True)) + lam_init)
    outs = []
    for h in range(N_HEADS):
        comp = []
        for c in range(2):
            acc = acc_scr[2 * h + c][...]
            comp.append(acc[0:A_DV] / acc[A_DV:A_DV + 1])
        o = comp[0] - lam * comp[1]
        ms = jnp.mean(o * o, axis=0, keepdims=True)
        outs.append(o * lax.rsqrt(ms + EPS) * g_ref[...] * (1.0 - lam_init))
    o_ref[0] = jnp.concatenate(outs, axis=0).T.astype(o_ref.dtype)


def _diff_attention(qta, ka, vta, a_lam, a_subln, lam_init, with_ctx):
    bsz, r, _ = ka.shape
    tq = ATT_TQ
    first = 0 if with_ctx else CTX // tq
    nq = r // tq - first
    return pl.pallas_call(
        partial(_diff_attn_kernel, first_block=first, lam_init=lam_init),
        grid=(bsz, nq),
        in_specs=[pl.BlockSpec((1, 256, tq), lambda b, i: (b, 0, i + first)),
                  pl.BlockSpec((1, r, 256), lambda b, i: (b, 0, 0)),
                  pl.BlockSpec((1, N_HEADS, A_VT, r), lambda b, i: (b, 0, 0, 0)),
                  pl.BlockSpec((4, A_DK), lambda b, i: (0, 0)),
                  pl.BlockSpec((A_DV, 1), lambda b, i: (0, 0))],
        out_specs=pl.BlockSpec((1, tq, 256), lambda b, i: (b, i, 0)),
        out_shape=jax.ShapeDtypeStruct((bsz, nq * tq, 256), BF16),
        scratch_shapes=([pltpu.VMEM((LANES, tq), BF16)] * N_CHAINS
                        + [pltpu.VMEM((A_VT, tq), F32)] * N_CHAINS
                        + [pltpu.VMEM((8, tq), F32)] * N_CHAINS),
        compiler_params=_cparams(("arbitrary", "arbitrary")),
        name="diff_attention",
    )(qta, ka, vta, a_lam, a_subln.reshape(A_DV, 1))


def _window_attn_kernel(q_ref, k_ref, vt_ref, sink_ref, o_ref, *, first_block):
    i = pl.program_id(1) + first_block
    tq = q_ref.shape[2]
    r = k_ref.shape[1]
    span = tq + 2 * WINDOW
    start = jnp.clip(i * tq - WINDOW, CTX, r - span)
    start = pl.multiple_of(start, WINDOW)
    kwin = k_ref[0, pl.ds(start, span), :]
    kctx = k_ref[0, 0:CTX, :]
    kpos = start + lax.broadcasted_iota(jnp.int32, (span, tq), 0)
    qpos = i * tq + lax.broadcasted_iota(jnp.int32, (span, tq), 1)
    valid = (jnp.abs(kpos - qpos) <= WINDOW) & (qpos >= CTX)
    sink = sink_ref[...] * LOG2E
    zeros = jnp.zeros((D_HD, tq), BF16)
    scores = []
    for h in range(N_HEADS):
        qh = q_ref[0, h * D_HD:(h + 1) * D_HD, :]
        wq = jnp.concatenate([qh, zeros] if h // 2 == 0 else [zeros, qh], axis=0)
        scores.append((_dot(kwin, wq), _dot(kctx, wq)))
    outs = []
    for h in range(N_HEADS):
        kv = h // 2
        s_loc = jnp.where(valid, scores[h][0], -jnp.inf)
        s_ctx = scores[h][1]
        sk = sink[0:1, h:h + 1]
        m = jnp.maximum(jnp.maximum(jnp.max(s_loc, axis=0, keepdims=True),
                                    jnp.max(s_ctx, axis=0, keepdims=True)), sk)
        p_loc = jnp.exp2(s_loc - m).astype(BF16)
        p_ctx = jnp.exp2(s_ctx - m).astype(BF16)
        acc = (_dot(vt_ref[0, kv, :, pl.ds(start, span)], p_loc)
               + _dot(vt_ref[0, kv, :, 0:CTX], p_ctx))
        outs.append(acc[0:D_HD] / (acc[D_HD:D_HD + 1] + jnp.exp2(sk - m)))
    o_ref[0] = jnp.concatenate(outs, axis=0).T.astype(o_ref.dtype)


def _window_attention(qdt, kd, vdt, sink, with_ctx):
    bsz, r, _ = kd.shape
    tq = ROW_TILE
    first = 0 if with_ctx else CTX // tq
    return pl.pallas_call(
        partial(_window_attn_kernel, first_block=first),
        grid=(bsz, r // tq - first),
        in_specs=[pl.BlockSpec((1, 256, tq), lambda b, i: (b, 0, i + first)),
                  pl.BlockSpec((1, r, 128), lambda b, i: (b, 0, 0)),
                  pl.BlockSpec((1, 2, A_VT, r), lambda b, i: (b, 0, 0, 0)),
                  pl.BlockSpec((1, LANES), lambda b, i: (0, 0))],
        out_specs=pl.BlockSpec((1, tq, 256), lambda b, i: (b, i, 0)),
        out_shape=jax.ShapeDtypeStruct((bsz, r - first * tq, 256), BF16),
        compiler_params=_cparams(("arbitrary", "arbitrary")),
        name="window_attention",
    )(qdt, kd, vdt, sink)


def _chunk_of_step(i, n_chunks, n_ctx_chunks, rev):
    if not rev:
        return i
    return jnp.where(i < n_ctx_chunks, n_ctx_chunks - 1 - i, n_chunks - 1 - (i - n_ctx_chunks))


def _tri(n, rev):
    t = lax.broadcasted_iota(jnp.int32, (n, n), 0)
    s = lax.broadcasted_iota(jnp.int32, (n, n), 1)
    return jnp.where((s >= t) if rev else (s <= t), 1.0, 0.0).astype(BF16)


def _round_robin(gens):
    live = list(gens)
    while live:
        still = []
        for g in live:
            try:
                next(g)
                still.append(g)
            except StopIteration:
                pass
        live = still


def _scan_call(chain_fn, p, consts, halo_w, out_w, state_shapes, name):
    bsz, r, pw = p.shape
    ln = SCAN_CHUNK
    nc, ncc = r // ln, CTX // ln
    per8 = ln // 8
    dirs = (False, True)

    def kern(*refs):
        n_halo = 4 if halo_w else 0
        p_refs, halo_refs = refs[0:2], refs[2:2 + n_halo]
        const_refs = refs[2 + n_halo:2 + n_halo + len(consts)]
        o_refs = refs[2 + n_halo + len(consts):4 + n_halo + len(consts)]
        scr = refs[4 + n_halo + len(consts):]
        i = pl.program_id(0)

        @pl.when(i == 0)
        def _():
            for s_ref in scr:
                s_ref[...] = jnp.zeros_like(s_ref)

        gens = []
        for b in range(bsz):
            for d, rev in enumerate(dirs):
                c = (b * 2 + d) * len(state_shapes)
                halos = (halo_refs[2 * d], halo_refs[2 * d + 1]) if halo_w else ()
                gens.append(chain_fn(p_refs[d], *halos, *const_refs, o_refs[d], *scr[c:c + len(state_shapes)],
                                     b=b, rev=rev, chunk=_chunk_of_step(i, nc, ncc, rev),
                                     n_chunks=nc, n_ctx_chunks=ncc))
        _round_robin(gens)

    chunk = lambda rev: (lambda i: _chunk_of_step(i, nc, ncc, rev))
    in_specs = [pl.BlockSpec((bsz, ln, pw), lambda i, f=chunk(rev): (0, f(i), 0)) for rev in dirs]
    args = [p, p]
    if halo_w:
        for rev in dirs:
            f = chunk(rev)
            in_specs.append(pl.BlockSpec((bsz, 8, halo_w), lambda i, f=f: (0, jnp.maximum(f(i) * per8 - 1, 0), 0)))
            in_specs.append(pl.BlockSpec((bsz, 8, halo_w),
                                         lambda i, f=f: (0, jnp.minimum((f(i) + 1) * per8, r // 8 - 1), 0)))
            args += [p, p]
    for cst in consts:
        in_specs.append(pl.BlockSpec(cst.shape, lambda i, n=cst.ndim: (0,) * n))
        args.append(cst)
    return pl.pallas_call(
        kern,
        grid=(nc,),
        in_specs=in_specs,
        out_specs=[pl.BlockSpec((bsz, ln, out_w), lambda i, f=chunk(rev): (0, f(i), 0)) for rev in dirs],
        out_shape=[jax.ShapeDtypeStruct((bsz, r, out_w), F32)] * 2,
        scratch_shapes=[pltpu.VMEM(shp, F32) for _ in range(2 * bsz) for shp in state_shapes],
        compiler_params=_cparams(("arbitrary",)),
        name=name,
    )(*args)


def _mlstm_chain(p_ref, prev_ref, next_ref, cw_ref, cb_ref, gb_ref, o_ref, ct_scr, n_scr, m_scr,
                 *, b, rev, chunk, n_chunks, n_ctx_chunks):
    ln = p_ref.shape[1]
    hw = N_HEADS * B_DH

    x = p_ref[b, :, 0:2 * hw]
    seg_first = (chunk == 0) | (chunk == n_ctx_chunks)
    seg_last = (chunk == n_ctx_chunks - 1) | (chunk == n_chunks - 1)
    row_prev = jnp.where(seg_first, 0.0, prev_ref[b, 7:8, :])
    row_next = jnp.where(seg_last, 0.0, next_ref[b, 0:1, :])
    ridx = lax.broadcasted_iota(jnp.int32, x.shape, 0)
    x_prev = jnp.where(ridx == 0, row_prev, pltpu.roll(x, 1, 0))
    x_next = jnp.where(ridx == ln - 1, row_next, pltpu.roll(x, ln - 1, 0))
    cw = cw_ref[...]
    y = cw[0:1] * x_prev + cw[1:2] * x + cw[2:3] * x_next + cb_ref[...]
    y = y * _sigmoid(y)
    q = y[:, 0:hw]
    k = y[:, hw:2 * hw] * (B_DH ** -0.5)
    v = p_ref[b, :, 2 * hw:3 * hw]

    g = p_ref[b, :, 4 * hw:4 * hw + LANES] + gb_ref[...]
    base = 8 if rev else 0
    end = 0 if rev else ln - 1
    tri = _tri(ln, rev)
    gt8 = g.T[base:base + 8]
    lf_r = _log_sigmoid(gt8)
    b_r = _dot_f32_by_bf16(lf_r, tri, nt=True)
    lf_c = pltpu.roll(_log_sigmoid(g), LANES - N_HEADS, 1)
    c_col = g - _dot_bf16_by_f32(tri, lf_c)
    s_idx = lax.broadcasted_iota(jnp.int32, (ln, ln), 0)
    t_idx = lax.broadcasted_iota(jnp.int32, (ln, ln), 1)
    causal = (s_idx >= t_idx) if rev else (s_idx <= t_idx)

    qb = q.astype(BF16)
    kb = k.astype(BF16)
    vb = v.astype(BF16)
    eye = jnp.where(lax.broadcasted_iota(jnp.int32, (hw, hw), 0) == lax.broadcasted_iota(jnp.int32, (hw, hw), 1),
                    1.0, 0.0).astype(BF16)
    qt = _dot_nt(eye, qb).astype(BF16)
    vt = _dot_nt(eye, vb)
    ct_old = ct_scr[...]
    n_old = n_scr[...]
    m_old = m_scr[0:1, :]
    nq_inter = _dot_f32_by_bf16(n_old, qt)
    rows_hw = lax.broadcasted_iota(jnp.int32, (hw, ln), 0)
    qtm = [jnp.where((rows_hw >= h * B_DH) & (rows_hw < (h + 1) * B_DH), qt, jnp.zeros_like(qt))
           for h in range(N_HEADS)]
    yield
    st = [_dot(kb, qtm[h]) for h in range(N_HEADS)]
    inter = [_dot(ct_old[h * B_DH:(h + 1) * B_DH].astype(BF16), qtm[h]) for h in range(N_HEADS)]
    yield

    h_parts, w_end_rows, decay_cols, decay8, w_end8 = [], [], [], [], []
    m_new_row = jnp.zeros((1, LANES), F32)
    lane128 = _lane_iota((1, LANES))
    for h in range(N_HEADS):
        col = c_col[:, base + h:base + h + 1]
        br_h = b_r[4 + h:5 + h, :]
        li_r = gt8[h:h + 1, :]
        m_h = m_old[:, h:h + 1]
        d = jnp.where(causal, col + br_h, -jnp.inf)
        m_inter = br_h + m_h
        m_t = jnp.maximum(m_inter, jnp.max(d, axis=0, keepdims=True))
        g_inter = jnp.exp(m_inter - m_t)
        s = st[h] * jnp.exp(d - m_t)
        num = _dot(vt[h * B_DH:(h + 1) * B_DH].astype(BF16), s.astype(BF16)) + g_inter * inter[h]
        yield
        nq = jnp.sum(s, axis=0, keepdims=True) + g_inter * nq_inter[h:h + 1]
        den = jnp.maximum(jnp.abs(nq), jnp.exp(-m_t))
        h_parts.append(num / den)
        b_end = br_h[:, end:end + 1]
        g_r = b_end - br_h + li_r
        m_new = jnp.maximum(b_end + m_h, jnp.max(g_r, axis=-1, keepdims=True))
        w_end = jnp.exp(g_r - m_new)
        decay = jnp.exp(b_end + m_h - m_new)
        w_end_rows.append(jnp.broadcast_to(w_end, (B_DH, ln)))
        decay_cols.append(jnp.broadcast_to(decay, (B_DH, 1)))
        w_end8.append(w_end)
        decay8.append(decay)
        m_new_row = jnp.where(lane128 == h, m_new, m_new_row)
    o_ref[b] = jnp.concatenate(h_parts, axis=0).T

    vtw = (vt * jnp.concatenate(w_end_rows, axis=0)).astype(BF16)
    ct_scr[...] = jnp.concatenate(decay_cols, axis=0) * ct_old + _dot(vtw, kb)
    pad4 = [jnp.zeros((1, ln), F32)] * (8 - N_HEADS)
    n_upd = _dot_f32_by_bf16(jnp.concatenate(w_end8 + pad4, axis=0), kb)
    own = _group_matrix(8, 1, hw, B_DH)
    n_scr[...] = (jnp.concatenate(decay8 + [jnp.zeros((1, 1), F32)] * (8 - N_HEADS), axis=0) * n_old
                  + own * n_upd)
    m_scr[0:1, :] = m_new_row


def _mlstm(pb, conv_w, conv_b, gate_b):
    hw = N_HEADS * B_DH
    return _scan_call(_mlstm_chain, pb, (conv_w, conv_b, gate_b), 2 * hw, hw,
                      ((hw, hw), (8, hw), (8, LANES)), "mlstm")


def _gla_chain(p_ref, wg_ref, bg_ref, o_ref, s_scr, *, b, rev, chunk, n_chunks, n_ctx_chunks):
    ln = p_ref.shape[1]
    kw_ = N_HEADS * C_DK
    vw_ = N_HEADS * C_DV
    c = GLA_SUB
    nb = ln // c
    z = 1 if rev else 0

    q = p_ref[b, :, 0:kw_] * (C_DK ** -0.5)
    k = p_ref[b, :, kw_:2 * kw_]
    v = p_ref[b, :, 2 * kw_:2 * kw_ + vw_]
    lr = p_ref[b, :, 2 * kw_ + vw_:2 * kw_ + vw_ + LANES]
    la = _log_sigmoid(_dot_bf16x3(lr, wg_ref[z]) + bg_ref[z]) * (1.0 / GLA_TAU)
    bc = _dot_bf16_by_f32(_tri(ln, rev), la)
    vb = v.astype(BF16)
    blk = _group_matrix(kw_, C_DK, vw_, C_DV)
    blk_b = blk.astype(BF16)

    s_old = s_scr[...]
    o_acc = _dot((q * jnp.exp(bc)).astype(BF16), s_old.astype(BF16))
    yield

    t_loc = lax.broadcasted_iota(jnp.int32, (c, kw_), 0)
    off_order = [jnp.where((t_loc >= t) if rev else (t_loc <= t), 0.0, NEG_BIG) for t in range(c)]
    pair_row = lax.broadcasted_iota(jnp.int32, (c, c * c), 0)
    pair_col = lax.broadcasted_iota(jnp.int32, (c, c * c), 1)
    sum_s = jnp.where((pair_col >= pair_row * c) & (pair_col < pair_row * c + c), 1.0, 0.0).astype(BF16)
    pair_sums = []
    for sub in range(nb):
        sl = slice(sub * c, (sub + 1) * c)
        bcb, qb_, kb_ = bc[sl], q[sl], k[sl]
        pieces = [(qb_[t:t + 1] * kb_) * jnp.exp(bcb[t:t + 1] - bcb + off_order[t]) for t in range(c)]
        e = jnp.concatenate(pieces, axis=0)
        pair_sums.append(_dot(e.astype(BF16), blk_b))
        if sub % 4 == 3:
            yield
    diag_parts = []
    for sub in range(nb):
        prod = (pair_sums[sub] * jnp.concatenate([v[sub * c:(sub + 1) * c]] * c, axis=0)).astype(BF16)
        diag_parts.append(_dot(sum_s, prod))
    yield
    o_acc = o_acc + jnp.concatenate(diag_parts, axis=0)

    rmask = lax.broadcasted_iota(jnp.int32, (N_HEADS * c, kw_), 0)
    lmask = _lane_iota((N_HEADS * c, kw_))
    rmask_v = lax.broadcasted_iota(jnp.int32, (N_HEADS * c, vw_), 0)
    lmask_v = _lane_iota((N_HEADS * c, vw_))
    same_k = jnp.zeros((N_HEADS * c, kw_), jnp.bool_)
    same_v = jnp.zeros((N_HEADS * c, vw_), jnp.bool_)
    for h in range(N_HEADS):
        same_k = same_k | ((rmask >= h * c) & (rmask < (h + 1) * c) & (lmask >= h * C_DK) & (lmask < (h + 1) * C_DK))
        same_v = same_v | ((rmask_v >= h * c) & (rmask_v < (h + 1) * c) & (lmask_v >= h * C_DV) & (lmask_v < (h + 1) * C_DV))
    off_full = jnp.zeros((ln, vw_), F32)
    for j in range(nb):
        if rev:
            if j == 0:
                continue
            rows = slice(0, j * c)
            edge = bc[j * c:j * c + 1]
        else:
            if j == nb - 1:
                continue
            rows = slice((j + 1) * c, ln)
            edge = bc[(j + 1) * c - 1:(j + 1) * c]
        sl = slice(j * c, (j + 1) * c)
        qs = q[rows] * jnp.exp(bc[rows] - edge)
        ks = k[sl] * jnp.exp(edge - bc[sl])
        kbd = jnp.where(same_k, jnp.concatenate([ks] * N_HEADS, axis=0), 0.0)
        vbd = jnp.where(same_v, jnp.concatenate([v[sl]] * N_HEADS, axis=0), 0.0)
        a = _dot_nt(qs.astype(BF16), kbd.astype(BF16))
        contrib = _dot(a.astype(BF16), vbd.astype(BF16))
        n_rows = contrib.shape[0]
        pad = jnp.zeros((ln - n_rows, vw_), F32)
        off_full = off_full + (jnp.concatenate([contrib, pad], axis=0) if rev
                               else jnp.concatenate([pad, contrib], axis=0))
        yield
    o_ref[b] = o_acc + off_full

    end = 0 if rev else ln - 1
    b_end = bc[end:end + 1]
    kd = k * jnp.exp(b_end - bc)
    upd = _dot(kd.T.astype(BF16), vb)
    eye = jnp.where(lax.broadcasted_iota(jnp.int32, (kw_, kw_), 0)
                    == lax.broadcasted_iota(jnp.int32, (kw_, kw_), 1), 1.0, 0.0).astype(F32)
    decay_col = jnp.sum(eye * jnp.exp(b_end), axis=-1, keepdims=True)
    s_scr[...] = decay_col * s_old + blk * upd


def _gla(pc, wg_pad, bg):
    return _scan_call(_gla_chain, pc, (wg_pad, bg), 0, N_HEADS * C_DV,
                      ((N_HEADS * C_DK, N_HEADS * C_DV),), "gla")


def _out_kernel(a_ref, bf_ref, bb_ref, bo_ref, cf_ref, cb_ref, d_ref, pg_ref, xc_ref, xl_ref, mod_ref,
                gn_ref, w_ref, o_ref, *, first_block):
    gmat = _group_matrix(GROUP_W, 64, GROUP_W, 64, 1.0 / 64).astype(BF16)
    gn = gn_ref[...]

    def rms64(hsum, gain):
        ms = _dot_f32_by_bf16(hsum * hsum, gmat)
        return hsum * lax.rsqrt(ms + EPS) * gain

    b_out = rms64(bf_ref[0] + bb_ref[0], gn[0:1]) * _sigmoid(bo_ref[0])
    c_out = rms64(cf_ref[0] + cb_ref[0], gn[1:2])
    y = jnp.concatenate([a_ref[0].astype(F32), b_out, c_out, d_ref[0].astype(F32)], axis=-1)
    gate = pg_ref[0].astype(F32)
    y = y * (gate * _sigmoid(gate))
    upd = _dot(y.astype(BF16), w_ref[...])
    x = _stream_tile(xc_ref, xl_ref, pl.program_id(1) + first_block)
    o_ref[0] = x + mod_ref[0, 0][2:3] * upd


def _out_projection(a_o, bf, bb, pb, cf, cb, d_o, pg, stream, msel, gains, w_out_b, with_ctx):
    bsz, r, d = pg.shape[0], pg.shape[1], msel.shape[-1]
    tm = ROW_TILE
    nct = CTX // tm
    first = 0 if with_ctx else nct
    row = lambda w: pl.BlockSpec((1, tm, w), lambda b, i: (b, i + first, 0))
    att = pl.BlockSpec((1, tm, 256), lambda b, i: (b, i, 0))
    out_rows = r - first * tm
    stream_specs, stream_args = _stream_specs(stream, tm, first)
    return pl.pallas_call(
        partial(_out_kernel, first_block=first),
        grid=(bsz, r // tm - first),
        in_specs=[att, row(256), row(256),
                  pl.BlockSpec((1, tm, 256), lambda b, i: (b, i + first, 3)),
                  row(256), row(256), att, row(PG_W), *stream_specs,
                  pl.BlockSpec((1, 1, 3, d), lambda b, i: (b, jnp.where(i + first >= nct, 1, 0), 0, 0)),
                  pl.BlockSpec((8, 256), lambda b, i: (0, 0)),
                  pl.BlockSpec((GROUP_W * 4, d), lambda b, i: (0, 0))],
        out_specs=pl.BlockSpec((1, tm, d), lambda b, i: (b, i, 0)),
        out_shape=jax.ShapeDtypeStruct((bsz, out_rows, d), F32),
        compiler_params=_cparams(("arbitrary", "arbitrary")),
        name="out_projection",
    )(a_o, bf, bb, pb, cf, cb, d_o, pg, *stream_args, msel, gains, w_out_b)


def _pad_cols(w, width):
    return jnp.pad(w, ((0, 0), (0, width - w.shape[1])))


def _layout_w_in(w):
    a = w[:, 0:768]
    bq, bk, bv = w[:, 768:1024], w[:, 1024:1280], w[:, 1280:1536]
    bg, bo = w[:, 1536:1552], w[:, 1552:1808]
    cq, ck, cv, clr = w[:, 1808:1936], w[:, 1936:2064], w[:, 2064:2320], w[:, 2320:2352]
    dd = w[:, 2352:2864]
    gg = w[:, 2864:3888]
    return jnp.concatenate([a, dd, bq, bk, bv, bo, _pad_cols(bg, LANES), cq, ck, cv, _pad_cols(clr, LANES), gg],
                           axis=1)


def _rope_tables(t, hd):
    nq = hd // 4
    per_tile = ROW_TILE // GRID_W
    inv = ROPE_THETA ** (-jnp.arange(nq, dtype=F32) / nq)
    lane = jnp.arange(256)
    within = (lane % hd) % (hd // 2)
    freq = inv[within % nq][None, :]
    sign = jnp.where(within < nq, -1.0, 1.0)[None, :]

    def tables(pos):
        ang = pos.astype(F32)[:, None] * freq
        return jnp.cos(ang), sign * jnp.sin(ang)

    row_cos, row_sin = tables(jnp.arange(t // GRID_W))
    col_cos, col_sin = tables(jnp.arange(GRID_W))

    def per_tile_blocks(tab, ctx_value):
        tab = jnp.pad(tab.reshape(-1, per_tile, 256), ((0, 0), (0, 8 - per_tile), (0, 0)))
        return jnp.concatenate([jnp.full((CTX // ROW_TILE, 8, 256), ctx_value, F32), tab], axis=0)

    return per_tile_blocks(row_cos, 1.0), per_tile_blocks(row_sin, 0.0), col_cos, col_sin


def _tile_lanes(g, width):
    return jnp.tile(g, width // g.shape[0])


def kernel(x, c, ctx, c_ctx, w_mod, b_mod, norm_g, w_in, w_out, a_qn, a_kn, a_lam, a_subln, b_conv_w, b_conv_b, b_gate_b, b_outn, c_wg, c_bg, c_outn, d_qn, d_kn, d_sink):
    bsz, t, d = x.shape
    depth = w_mod.shape[0]
    assert d == D_MODEL and ctx.shape[1] == CTX and t % ROW_TILE == 0 and bsz + 1 <= 8

    s_in = jnp.zeros((8, d), F32).at[:bsz].set(c).at[bsz].set(c_ctx)
    mod = _modulation(s_in, w_mod, b_mod).reshape(depth, 8, 3, d)
    tabs = _rope_tables(t, A_DK) + _rope_tables(t, D_HD)

    xc = (ctx, x)
    for l in range(depth):
        with_ctx = l < depth - 1
        lam_init = 0.8 - 0.6 * math.exp(-0.3 * l)
        msel = jnp.stack([jnp.broadcast_to(mod[l, bsz], (bsz, 3, d)), mod[l, :bsz]], axis=1)
        w_in_p = _layout_w_in(w_in[l]).astype(BF16)
        gains_att = jnp.zeros((8, 256), F32)
        gains_att = gains_att.at[0].set(_tile_lanes(a_qn[l], 256)).at[1].set(_tile_lanes(a_kn[l], 256))
        gains_att = gains_att.at[2].set(_tile_lanes(d_qn[l], 256)).at[3].set(_tile_lanes(d_kn[l], 256))
        pb, pc, pg, qta, ka, vta, qdt, kd, vdt = _in_projection(xc, msel, norm_g[l], w_in_p, tabs, gains_att)
        a_o = _diff_attention(qta, ka, vta, a_lam[l], a_subln[l], lam_init, with_ctx)
        sink = jnp.zeros((1, LANES), F32).at[0, :N_HEADS].set(d_sink[l])
        d_o = _window_attention(qdt, kd, vdt, sink, with_ctx)

        gate_b = jnp.zeros((1, LANES), F32).at[0, :16].set(b_gate_b[l].reshape(16))
        conv_b = b_conv_b[l].reshape(1, -1)
        bf, bb = _mlstm(pb, b_conv_w[l], conv_b, gate_b)

        wg_pad = jnp.zeros((2, LANES, LANES), F32)
        for z in range(2):
            wg_pad = wg_pad.at[z, z * GLA_RANK:(z + 1) * GLA_RANK].set(c_wg[l, z])
        cf, cb = _gla(pc, wg_pad, c_bg[l].reshape(2, 1, LANES))

        gains_out = jnp.zeros((8, 256), F32)
        gains_out = gains_out.at[0].set(_tile_lanes(b_outn[l], 256)).at[1].set(_tile_lanes(c_outn[l], 256))
        xc = _out_projection(a_o, bf, bb, pb, cf, cb, d_o, pg, xc, msel, gains_out,
                             w_out[l].astype(BF16), with_ctx)
    return xc
```

```python
import math
from functools import partial

import jax
import jax.numpy as jnp
from jax import lax
from jax.experimental import pallas as pl
from jax.experimental.pallas import tpu as pltpu

F32 = jnp.float32
BF16 = jnp.bfloat16
HI = lax.Precision.HIGHEST

D_MODEL = 1024
CTX = 256
GRID_W = 64
GROUP_W = 256
N_HEADS = 4
A_DK = 32
A_DV = 64
B_DH = 64
C_DK = 32
C_DV = 64
GLA_RANK = 16
GLA_TAU = 16.0
D_HD = 64
WINDOW = 128
ROPE_THETA = 10000.0
EPS = 1e-6

LANES = 128
VMEM_LIMIT = 56 * 1024 * 1024

PA_W = 768
PB_W = 1152
PC_W = 640
PD_W = 512
PG_W = 1024
P_W = PA_W + PB_W + PC_W + PD_W + PG_W

ROW_TILE = 256
SCAN_CHUNK = 128
GLA_SUB = 16
ATT_TQ = 256
ATT_TK = 512
ATT_TILES_PER_STEP = 16
ATT_SUM_LIMIT = 2.0 ** 40

NEG_BIG = -1e30
LOG2E = 1.4426950408889634
A_VT = 80
N_CHAINS = 2 * N_HEADS
ATT_SKEW = 8


def _cparams(sem):
    return pltpu.CompilerParams(dimension_semantics=sem, vmem_limit_bytes=VMEM_LIMIT)


def _log_sigmoid(x):
    return jnp.minimum(x, 0.0) - jnp.log(1.0 + jnp.exp(-jnp.abs(x)))


def _sigmoid(x):
    return 1.0 / (1.0 + jnp.exp(-x))


def _lane_iota(shape):
    return lax.broadcasted_iota(jnp.int32, shape, len(shape) - 1)


def _head_mask(width, per_head, h, rows=1):
    lane = _lane_iota((rows, width))
    return (lane >= h * per_head) & (lane < (h + 1) * per_head)


def _group_matrix(n_rows, rows_per_group, n_cols, cols_per_group, value=1.0):
    r = lax.broadcasted_iota(jnp.int32, (n_rows, n_cols), 0)
    c = lax.broadcasted_iota(jnp.int32, (n_rows, n_cols), 1)
    same = jnp.zeros((n_rows, n_cols), jnp.bool_)
    for g in range(n_rows // rows_per_group):
        same = same | ((r >= g * rows_per_group) & (r < (g + 1) * rows_per_group)
                       & (c >= g * cols_per_group) & (c < (g + 1) * cols_per_group))
    return jnp.where(same, value, 0.0).astype(F32)


def _dot(a, b, precision=None):
    return jnp.dot(a, b, preferred_element_type=F32, precision=precision)


def _dot_bf16x3(a, b):
    a_hi = a.astype(BF16)
    a_lo = (a - a_hi.astype(F32)).astype(BF16)
    b_hi = b.astype(BF16)
    b_lo = (b - b_hi.astype(F32)).astype(BF16)
    return _dot(a_hi, b_hi) + _dot(a_hi, b_lo) + _dot(a_lo, b_hi)


def _split3(x):
    hi = x.astype(BF16)
    r1 = x - hi.astype(F32)
    mid = r1.astype(BF16)
    lo = (r1 - mid.astype(F32)).astype(BF16)
    return hi, mid, lo


def _dot_f32_by_bf16(a, b_bf16, nt=False):
    f = _dot_nt if nt else _dot
    hi, mid, lo = _split3(a)
    return f(hi, b_bf16) + f(mid, b_bf16) + f(lo, b_bf16)


def _dot_bf16_by_f32(a_bf16, b):
    hi, mid, lo = _split3(b)
    return _dot(a_bf16, hi) + _dot(a_bf16, mid) + _dot(a_bf16, lo)


def _dot_nt(a, b, precision=None):
    return lax.dot_general(a, b, (((1,), (1,)), ((), ())), preferred_element_type=F32,
                           precision=precision)


def _mod_kernel(s_ref, w_ref, b_ref, o_ref):
    s = s_ref[...]
    s = s * _sigmoid(s)
    o_ref[0] = _dot(s, w_ref[0], HI) + b_ref[0]


def _modulation(s_in, w_mod, b_mod):
    depth, d, n = w_mod.shape
    tn = 1024
    return pl.pallas_call(
        _mod_kernel,
        grid=(depth, n // tn),
        in_specs=[pl.BlockSpec((8, d), lambda l, j: (0, 0)),
                  pl.BlockSpec((1, d, tn), lambda l, j: (l, 0, j)),
                  pl.BlockSpec((1, 1, tn), lambda l, j: (l, 0, j))],
        out_specs=pl.BlockSpec((1, 8, tn), lambda l, j: (l, 0, j)),
        out_shape=jax.ShapeDtypeStruct((depth, 8, n), F32),
        compiler_params=_cparams(("arbitrary", "arbitrary")),
        name="modulation",
    )(s_in, w_mod, b_mod.reshape(depth, 1, n))


def _stream_specs(stream, tm, first=0):
    nct = CTX // tm
    if isinstance(stream, tuple):
        ctx, lat = stream
        off = nct
    else:
        ctx = lat = stream
        off = 0
    d = lat.shape[-1]
    return ([pl.BlockSpec((1, tm, d), lambda b, i: (b, jnp.minimum(i + first, nct - 1), 0)),
             pl.BlockSpec((1, tm, d), lambda b, i: (b, jnp.maximum(i + first - off, 0), 0))], [ctx, lat])


def _stream_tile(ctx_ref, lat_ref, tile):
    return jnp.where(tile < CTX // ctx_ref.shape[1], ctx_ref[0], lat_ref[0])


def _proj_kernel(xc_ref, xl_ref, mod_ref, g_ref, w_ref, *refs):
    tab_refs, (pb_ref, pc_ref, pg_ref), att_refs = refs[0:9], refs[9:12], refs[12:18]
    x = _stream_tile(xc_ref, xl_ref, pl.program_id(1))
    ms = jnp.mean(x * x, axis=-1, keepdims=True)
    y = x * lax.rsqrt(ms + EPS) * g_ref[...]
    mod = mod_ref[0, 0]
    hx = y * (1.0 + mod[1:2]) + mod[0:1]
    hb = hx.astype(BF16)
    n_ad = PA_W + PD_W
    p_ad = _dot(hb, w_ref[:, 0:n_ad])
    _prep_attn(p_ad[:, 0:PA_W], p_ad[:, PA_W:n_ad], *tab_refs, *att_refs)
    p = _dot(hb, w_ref[:, n_ad:P_W])
    pb_ref[0] = p[:, 0:PB_W]
    pc_ref[0] = p[:, PB_W:PB_W + PC_W]
    pg_ref[0] = p[:, PB_W + PC_W:PB_W + PC_W + PG_W].astype(BF16)


def _in_projection(stream, msel, norm_g, w_in_p, tabs, gains):
    bsz, d = msel.shape[0], msel.shape[-1]
    r = sum(a.shape[1] for a in stream) if isinstance(stream, tuple) else stream.shape[1]
    tm = ROW_TILE
    nct = CTX // tm
    row = lambda w: pl.BlockSpec((1, tm, w), lambda b, i: (b, i, 0))
    col = lambda h: pl.BlockSpec((1, h, tm), lambda b, i: (b, 0, i))
    vt = lambda n, rows: pl.BlockSpec((1, n, rows, tm), lambda b, i: (b, 0, 0, i))
    rtab = pl.BlockSpec((1, 8, 256), lambda b, i: (i, 0, 0))
    ctab = pl.BlockSpec((GRID_W, 256), lambda b, i: (0, 0))
    stream_specs, stream_args = _stream_specs(stream, tm)
    return pl.pallas_call(
        _proj_kernel,
        grid=(bsz, r // tm),
        in_specs=[*stream_specs,
                  pl.BlockSpec((1, 1, 3, d), lambda b, i: (b, jnp.where(i >= nct, 1, 0), 0, 0)),
                  pl.BlockSpec((1, d), lambda b, i: (0, 0)),
                  pl.BlockSpec((d, P_W), lambda b, i: (0, 0)),
                  rtab, rtab, ctab, ctab, rtab, rtab, ctab, ctab,
                  pl.BlockSpec((8, 256), lambda b, i: (0, 0))],
        out_specs=[row(PB_W), row(PC_W), row(PG_W),
                   col(256), row(256), vt(N_HEADS, A_VT), col(256), row(128), vt(2, A_VT)],
        out_shape=[jax.ShapeDtypeStruct((bsz, r, PB_W), F32),
                   jax.ShapeDtypeStruct((bsz, r, PC_W), F32),
                   jax.ShapeDtypeStruct((bsz, r, PG_W), BF16),
                   jax.ShapeDtypeStruct((bsz, 256, r), BF16),
                   jax.ShapeDtypeStruct((bsz, r, 256), BF16),
                   jax.ShapeDtypeStruct((bsz, N_HEADS, A_VT, r), BF16),
                   jax.ShapeDtypeStruct((bsz, 256, r), BF16),
                   jax.ShapeDtypeStruct((bsz, r, 128), BF16),
                   jax.ShapeDtypeStruct((bsz, 2, A_VT, r), BF16)],
        compiler_params=_cparams(("arbitrary", "arbitrary")),
        name="in_projection",
    )(*stream_args, msel, norm_g.reshape(1, d), w_in_p, *tabs, gains)


def _norm_rope(x, gain, cos, sin, group, half_pair, scale):
    w = x.shape[-1]
    gmat = _group_matrix(w, group, w, group, 1.0 / group).astype(BF16)
    ms = _dot_f32_by_bf16(x * x, gmat)
    xn = x * lax.rsqrt(ms + EPS) * gain
    lane = _lane_iota(xn.shape)
    first = (lane & (2 * half_pair - 1)) < half_pair
    partner = jnp.where(first, pltpu.roll(xn, w - half_pair, 1), pltpu.roll(xn, half_pair, 1))
    out = xn * cos + partner * sin
    return out * scale if scale != 1.0 else out


def _expand_rope(row_ref, col_ref, use_col):
    per_tile = ROW_TILE // GRID_W
    rt = row_ref[0]
    by_row = jnp.concatenate([jnp.broadcast_to(rt[j:j + 1], (GRID_W, 256)) for j in range(per_tile)], axis=0)
    by_col = jnp.concatenate([col_ref[...]] * per_tile, axis=0)
    return jnp.where(use_col, by_col, by_row)


def _prep_attn(pa, pd, rcosa_ref, rsina_ref, ccosa_ref, csina_ref,
               rcosd_ref, rsind_ref, ccosd_ref, csind_ref, gains_ref,
               qta_ref, ka_ref, vta_ref, qdt_ref, kd_ref, vdt_ref):
    latent = pl.program_id(1) >= CTX // ROW_TILE
    lane = _lane_iota((ROW_TILE, 256))
    col_a = ((lane & (A_DK // 2)) != 0) & latent
    col_d = ((lane & (D_HD // 2)) != 0) & latent
    cosa, sina = _expand_rope(rcosa_ref, ccosa_ref, col_a), _expand_rope(rsina_ref, csina_ref, col_a)
    gains = gains_ref[...]
    qa = _norm_rope(pa[:, 0:256], gains[0:1], cosa, sina, A_DK, A_DK // 4, A_DK ** -0.5 * LOG2E)
    ka = _norm_rope(pa[:, 256:512], gains[1:2], cosa, sina, A_DK, A_DK // 4, 1.0)
    qta_ref[0] = qa.T.astype(BF16)
    ka_ref[0] = ka.astype(BF16)
    vat = pa[:, 512:768].T
    rows = vat.shape[1]
    ones_rows = jnp.where(lax.broadcasted_iota(jnp.int32, (A_VT - A_DV, rows), 0) == 0, 1.0, 0.0).astype(BF16)
    for h in range(N_HEADS):
        vta_ref[0, h, 0:A_DV, :] = vat[h * A_DV:(h + 1) * A_DV].astype(BF16)
        vta_ref[0, h, A_DV:A_VT, :] = ones_rows
    cosd, sind = _expand_rope(rcosd_ref, ccosd_ref, col_d), _expand_rope(rsind_ref, csind_ref, col_d)
    qd = _norm_rope(pd[:, 0:256], gains[2:3], cosd, sind, D_HD, D_HD // 4, D_HD ** -0.5 * LOG2E)
    kd = _norm_rope(pd[:, 256:384], gains[3:4, 0:128], cosd[:, 0:128], sind[:, 0:128], D_HD, D_HD // 4, 1.0)
    qdt_ref[0] = qd.T.astype(BF16)
    kd_ref[0] = kd.astype(BF16)
    vdt = pd[:, 384:512].T
    for kv in range(2):
        vdt_ref[0, kv, 0:D_HD, :] = vdt[kv * D_HD:(kv + 1) * D_HD].astype(BF16)
        vdt_ref[0, kv, D_HD:A_VT, :] = ones_rows


def _diff_attn_kernel(qt_ref, k_ref, vt_ref, lam_ref, g_ref, o_ref, *scr, first_block, lam_init):
    wq_scr, acc_scr, m_scr = scr[0:N_CHAINS], scr[N_CHAINS:2 * N_CHAINS], scr[2 * N_CHAINS:3 * N_CHAINS]
    i = pl.program_id(1) + first_block
    tq = qt_ref.shape[2]
    n_tiles = (k_ref.shape[1] - CTX) // ATT_TK
    tiles_per_step = min(ATT_TILES_PER_STEP, n_tiles)
    step_keys = ATT_TK * tiles_per_step
    n_steps = jnp.where(i == 0, 0, n_tiles // tiles_per_step)
    row = lax.broadcasted_iota(jnp.int32, (LANES, tq), 0)
    for ch in range(N_CHAINS):
        grp, r_in = ch // 4, (ch % 4) * A_DK
        qt = qt_ref[0, grp * LANES:(grp + 1) * LANES, :]
        wq_scr[ch][...] = jnp.where((row >= r_in) & (row < r_in + A_DK), qt, jnp.zeros_like(qt))
        m_scr[ch][...] = jnp.full((8, tq), NEG_BIG, F32)
        acc_scr[ch][...] = jnp.zeros((A_VT, tq), F32)

    def process(tiles):
        units = [(ch, off, size) for off, size in tiles for ch in range(N_CHAINS)]

        def scores(u):
            ch, off, size = units[u]
            grp = ch // 4
            kt = k_ref[0, pl.ds(off, size), grp * LANES:(grp + 1) * LANES]
            return _dot(kt, wq_scr[ch][...])

        s = {u: scores(u) for u in range(min(ATT_SKEW, len(units)))}
        for u, (ch, off, size) in enumerate(units):
            m_old = m_scr[ch][0:1, :]
            m_new = jnp.maximum(m_old, jnp.max(s[u], axis=0, keepdims=True))
            alpha = jnp.exp2(m_old - m_new)
            p = jnp.exp2(s.pop(u) - m_new).astype(BF16)
            if u + ATT_SKEW < len(units):
                s[u + ATT_SKEW] = scores(u + ATT_SKEW)
            pv = _dot(vt_ref[0, ch // 2, :, pl.ds(off, size)], p)
            acc_scr[ch][...] = alpha * acc_scr[ch][...] + pv
            m_scr[ch][...] = jnp.broadcast_to(m_new, (8, tq))

    def process_fixed_max(tiles):
        units = [(ch, off, size) for off, size in tiles for ch in range(N_CHAINS)]

        def scores(u):
            ch, off, size = units[u]
            grp = ch // 4
            kt = k_ref[0, pl.ds(off, size), grp * LANES:(grp + 1) * LANES]
            return _dot(kt, wq_scr[ch][...])

        s = {u: scores(u) for u in range(min(ATT_SKEW, len(units)))}
        pv = [None] * N_CHAINS
        for u, (ch, off, size) in enumerate(units):
            p = jnp.exp2(s.pop(u) - m_scr[ch][0:1, :]).astype(BF16)
            if u + ATT_SKEW < len(units):
                s[u + ATT_SKEW] = scores(u + ATT_SKEW)
            d = _dot(vt_ref[0, ch // 2, :, pl.ds(off, size)], p)
            pv[ch] = d if pv[ch] is None else pv[ch] + d
        worst = None
        for ch in range(N_CHAINS):
            row_sum = acc_scr[ch][A_DV:A_DV + 1, :] + pv[ch][A_DV:A_DV + 1, :]
            worst = row_sum if worst is None else jnp.maximum(worst, row_sum)
        ok = jnp.max(worst) < ATT_SUM_LIMIT

        def commit():
            for ch in range(N_CHAINS):
                acc_scr[ch][...] = acc_scr[ch][...] + pv[ch]

        lax.cond(ok, commit, lambda: process_tiles(tiles[0][0], len(tiles)))

    def tile_offset(first, t):
        return pl.multiple_of(first + t * ATT_TK, math.gcd(CTX, ATT_TK))

    def process_tiles(first, n):
        def one(t, carry):
            process([(tile_offset(first, t), ATT_TK)])
            return carry
        lax.fori_loop(0, n, one, 0)

    process([(0, CTX)])

    def step(j, carry):
        process_fixed_max([(tile_offset(CTX + j * step_keys, t), ATT_TK) for t in range(tiles_per_step)])
        return carry

    lax.fori_loop(0, n_steps, step, 0)
    process_tiles(CTX + n_steps * step_keys, jnp.where(i == 0, 0, n_tiles % tiles_per_step))

    lp = lam_ref[...]
    lam = (jnp.exp(jnp.sum(lp[0:1] * lp[1:2], axis=-1, keepdims=True))
           - jnp.exp(jnp.sum(lp[2:3] * lp[3:4], axis=-1, keepdims=True)) + lam_init)
    outs = []
    for h in range(N_HEADS):
        comp = []
        for c in range(2):
            acc = acc_scr[2 * h + c][...]
            comp.append(acc[0:A_DV] / acc[A_DV:A_DV + 1])
        o = comp[0] - lam * comp[1]
        ms = jnp.mean(o * o, axis=0, keepdims=True)
        outs.append(o * lax.rsqrt(ms + EPS) * g_ref[...] * (1.0 - lam_init))
    o_ref[0] = jnp.concatenate(outs, axis=0).T.astype(o_ref.dtype)


def _diff_attention(qta, ka, vta, a_lam, a_subln, lam_init, with_ctx):
    bsz, r, _ = ka.shape
    tq = ATT_TQ
    first = 0 if with_ctx else CTX // tq
    nq = r // tq - first
    return pl.pallas_call(
        partial(_diff_attn_kernel, first_block=first, lam_init=lam_init),
        grid=(bsz, nq),
        in_specs=[pl.BlockSpec((1, 256, tq), lambda b, i: (b, 0, i + first)),
                  pl.BlockSpec((1, r, 256), lambda b, i: (b, 0, 0)),
                  pl.BlockSpec((1, N_HEADS, A_VT, r), lambda b, i: (b, 0, 0, 0)),
                  pl.BlockSpec((4, A_DK), lambda b, i: (0, 0)),
                  pl.BlockSpec((A_DV, 1), lambda b, i: (0, 0))],
        out_specs=pl.BlockSpec((1, tq, 256), lambda b, i: (b, i, 0)),
        out_shape=jax.ShapeDtypeStruct((bsz, nq * tq, 256), BF16),
        scratch_shapes=([pltpu.VMEM((LANES, tq), BF16)] * N_CHAINS
                        + [pltpu.VMEM((A_VT, tq), F32)] * N_CHAINS
                        + [pltpu.VMEM((8, tq), F32)] * N_CHAINS),
        compiler_params=_cparams(("arbitrary", "arbitrary")),
        name="diff_attention",
    )(qta, ka, vta, a_lam, a_subln.reshape(A_DV, 1))


def _window_attn_kernel(q_ref, k_ref, vt_ref, sink_ref, o_ref, *, first_block):
    i = pl.program_id(1) + first_block
    tq = q_ref.shape[2]
    r = k_ref.shape[1]
    span = tq + 2 * WINDOW
    start = jnp.clip(i * tq - WINDOW, CTX, r - span)
    start = pl.multiple_of(start, WINDOW)
    kwin = k_ref[0, pl.ds(start, span), :]
    kctx = k_ref[0, 0:CTX, :]
    kpos = start + lax.broadcasted_iota(jnp.int32, (span, tq), 0)
    qpos = i * tq + lax.broadcasted_iota(jnp.int32, (span, tq), 1)
    valid = (jnp.abs(kpos - qpos) <= WINDOW) & (qpos >= CTX)
    sink = sink_ref[...] * LOG2E
    zeros = jnp.zeros((D_HD, tq), BF16)
    scores = []
    for h in range(N_HEADS):
        qh = q_ref[0, h * D_HD:(h + 1) * D_HD, :]
        wq = jnp.concatenate([qh, zeros] if h // 2 == 0 else [zeros, qh], axis=0)
        scores.append((_dot(kwin, wq), _dot(kctx, wq)))
    outs = []
    for h in range(N_HEADS):
        kv = h // 2
        s_loc = jnp.where(valid, scores[h][0], -jnp.inf)
        s_ctx = scores[h][1]
        sk = sink[0:1, h:h + 1]
        m = jnp.maximum(jnp.maximum(jnp.max(s_loc, axis=0, keepdims=True),
                                    jnp.max(s_ctx, axis=0, keepdims=True)), sk)
        p_loc = jnp.exp2(s_loc - m).astype(BF16)
        p_ctx = jnp.exp2(s_ctx - m).astype(BF16)
        acc = (_dot(vt_ref[0, kv, :, pl.ds(start, span)], p_loc)
               + _dot(vt_ref[0, kv, :, 0:CTX], p_ctx))
        outs.append(acc[0:D_HD] / (acc[D_HD:D_HD + 1] + jnp.exp2(sk - m)))
    o_ref[0] = jnp.concatenate(outs, axis=0).T.astype(o_ref.dtype)


def _window_attention(qdt, kd, vdt, sink, with_ctx):
    bsz, r, _ = kd.shape
    tq = ROW_TILE
    first = 0 if with_ctx else CTX // tq
    return pl.pallas_call(
        partial(_window_attn_kernel, first_block=first),
        grid=(bsz, r // tq - first),
        in_specs=[pl.BlockSpec((1, 256, tq), lambda b, i: (b, 0, i + first)),
                  pl.BlockSpec((1, r, 128), lambda b, i: (b, 0, 0)),
                  pl.BlockSpec((1, 2, A_VT, r), lambda b, i: (b, 0, 0, 0)),
                  pl.BlockSpec((1, LANES), lambda b, i: (0, 0))],
        out_specs=pl.BlockSpec((1, tq, 256), lambda b, i: (b, i, 0)),
        out_shape=jax.ShapeDtypeStruct((bsz, r - first * tq, 256), BF16),
        compiler_params=_cparams(("arbitrary", "arbitrary")),
        name="window_attention",
    )(qdt, kd, vdt, sink)


def _chunk_of_step(i, n_chunks, n_ctx_chunks, rev):
    if not rev:
        return i
    return jnp.where(i < n_ctx_chunks, n_ctx_chunks - 1 - i, n_chunks - 1 - (i - n_ctx_chunks))


def _tri(n, rev):
    t = lax.broadcasted_iota(jnp.int32, (n, n), 0)
    s = lax.broadcasted_iota(jnp.int32, (n, n), 1)
    return jnp.where((s >= t) if rev else (s <= t), 1.0, 0.0).astype(BF16)


def _round_robin(gens):
    live = list(gens)
    while live:
        still = []
        for g in live:
            try:
                next(g)
                still.append(g)
            except StopIteration:
                pass
        live = still


def _scan_call(chain_fn, p, consts, halo_w, out_w, state_shapes, name):
    bsz, r, pw = p.shape
    ln = SCAN_CHUNK
    nc, ncc = r // ln, CTX // ln
    per8 = ln // 8
    dirs = (False, True)

    def kern(*refs):
        n_halo = 4 if halo_w else 0
        p_refs, halo_refs = refs[0:2], refs[2:2 + n_halo]
        const_refs = refs[2 + n_halo:2 + n_halo + len(consts)]
        o_refs = refs[2 + n_halo + len(consts):4 + n_halo + len(consts)]
        scr = refs[4 + n_halo + len(consts):]
        i = pl.program_id(0)

        @pl.when(i == 0)
        def _():
            for s_ref in scr:
                s_ref[...] = jnp.zeros_like(s_ref)

        gens = []
        for b in range(bsz):
            for d, rev in enumerate(dirs):
                c = (b * 2 + d) * len(state_shapes)
                halos = (halo_refs[2 * d], halo_refs[2 * d + 1]) if halo_w else ()
                gens.append(chain_fn(p_refs[d], *halos, *const_refs, o_refs[d], *scr[c:c + len(state_shapes)],
                                     b=b, rev=rev, chunk=_chunk_of_step(i, nc, ncc, rev),
                                     n_chunks=nc, n_ctx_chunks=ncc))
        _round_robin(gens)

    chunk = lambda rev: (lambda i: _chunk_of_step(i, nc, ncc, rev))
    in_specs = [pl.BlockSpec((bsz, ln, pw), lambda i, f=chunk(rev): (0, f(i), 0)) for rev in dirs]
    args = [p, p]
    if halo_w:
        for rev in dirs:
            f = chunk(rev)
            in_specs.append(pl.BlockSpec((bsz, 8, halo_w), lambda i, f=f: (0, jnp.maximum(f(i) * per8 - 1, 0), 0)))
            in_specs.append(pl.BlockSpec((bsz, 8, halo_w),
                                         lambda i, f=f: (0, jnp.minimum((f(i) + 1) * per8, r // 8 - 1), 0)))
            args += [p, p]
    for cst in consts:
        in_specs.append(pl.BlockSpec(cst.shape, lambda i, n=cst.ndim: (0,) * n))
        args.append(cst)
    return pl.pallas_call(
        kern,
        grid=(nc,),
        in_specs=in_specs,
        out_specs=[pl.BlockSpec((bsz, ln, out_w), lambda i, f=chunk(rev): (0, f(i), 0)) for rev in dirs],
        out_shape=[jax.ShapeDtypeStruct((bsz, r, out_w), F32)] * 2,
        scratch_shapes=[pltpu.VMEM(shp, F32) for _ in range(2 * bsz) for shp in state_shapes],
        compiler_params=_cparams(("arbitrary",)),
        name=name,
    )(*args)


def _mlstm_chain(p_ref, prev_ref, next_ref, cw_ref, cb_ref, gb_ref, o_ref, ct_scr, n_scr, m_scr,
                 *, b, rev, chunk, n_chunks, n_ctx_chunks):
    ln = p_ref.shape[1]
    hw = N_HEADS * B_DH

    x = p_ref[b, :, 0:2 * hw]
    seg_first = (chunk == 0) | (chunk == n_ctx_chunks)
    seg_last = (chunk == n_ctx_chunks - 1) | (chunk == n_chunks - 1)
    row_prev = jnp.where(seg_first, 0.0, prev_ref[b, 7:8, :])
    row_next = jnp.where(seg_last, 0.0, next_ref[b, 0:1, :])
    ridx = lax.broadcasted_iota(jnp.int32, (8, x.shape[1]), 0)
    down, up = pltpu.roll(x, 1, 0), pltpu.roll(x, ln - 1, 0)
    x_prev = jnp.concatenate([jnp.where(ridx == 0, row_prev, down[0:8]), down[8:]], axis=0)
    x_next = jnp.concatenate([up[:ln - 8], jnp.where(ridx == 7, row_next, up[ln - 8:])], axis=0)
    cw = cw_ref[...]
    y = cw[0:1] * x_prev + cw[1:2] * x + cw[2:3] * x_next + cb_ref[...]
    y = y * _sigmoid(y)
    q = y[:, 0:hw]
    k = y[:, hw:2 * hw] * (B_DH ** -0.5)
    v = p_ref[b, :, 2 * hw:3 * hw]

    g = p_ref[b, :, 4 * hw:4 * hw + LANES] + gb_ref[...]
    base = 8 if rev else 0
    end = 0 if rev else ln - 1
    tri = _tri(ln, rev)
    gt8 = g.T[base:base + 8]
    lf_r = _log_sigmoid(gt8)
    b_r = _dot_f32_by_bf16(lf_r, tri, nt=True)
    lf_c = pltpu.roll(_log_sigmoid(g), LANES - N_HEADS, 1)
    c_col = g - _dot_bf16_by_f32(tri, lf_c)
    s_idx = lax.broadcasted_iota(jnp.int32, (ln, ln), 0)
    t_idx = lax.broadcasted_iota(jnp.int32, (ln, ln), 1)
    causal = (s_idx >= t_idx) if rev else (s_idx <= t_idx)

    qb = q.astype(BF16)
    kb = k.astype(BF16)
    vb = v.astype(BF16)
    eye = jnp.where(lax.broadcasted_iota(jnp.int32, (hw, hw), 0) == lax.broadcasted_iota(jnp.int32, (hw, hw), 1),
                    1.0, 0.0).astype(BF16)
    qt = _dot_nt(eye, qb).astype(BF16)
    vt = _dot_nt(eye, vb)
    ct_old = ct_scr[...]
    n_old = n_scr[...]
    m_old = m_scr[0:1, :]
    nq_inter = _dot_f32_by_bf16(n_old, qt)
    rows_hw = lax.broadcasted_iota(jnp.int32, (hw, ln), 0)
    qtm = [jnp.where((rows_hw >= h * B_DH) & (rows_hw < (h + 1) * B_DH), qt, jnp.zeros_like(qt))
           for h in range(N_HEADS)]
    yield
    st = [_dot(kb, qtm[h]) for h in range(N_HEADS)]
    inter = [_dot(ct_old[h * B_DH:(h + 1) * B_DH].astype(BF16), qtm[h]) for h in range(N_HEADS)]
    yield

    h_parts, w_end_rows, decay_cols, decay8, w_end8 = [], [], [], [], []
    m_new_row = jnp.zeros((1, LANES), F32)
    lane128 = _lane_iota((1, LANES))
    for h in range(N_HEADS):
        col = c_col[:, base + h:base + h + 1]
        br_h = b_r[4 + h:5 + h, :]
        li_r = gt8[h:h + 1, :]
        m_h = m_old[:, h:h + 1]
        d = jnp.where(causal, col + br_h, -jnp.inf)
        m_inter = br_h + m_h
        m_t = jnp.maximum(m_inter, jnp.max(d, axis=0, keepdims=True))
        g_inter = jnp.exp(m_inter - m_t)
        s = st[h] * jnp.exp(d - m_t)
        num = _dot(vt[h * B_DH:(h + 1) * B_DH].astype(BF16), s.astype(BF16)) + g_inter * inter[h]
        yield
        nq = jnp.sum(s, axis=0, keepdims=True) + g_inter * nq_inter[h:h + 1]
        den = jnp.maximum(jnp.abs(nq), jnp.exp(-m_t))
        h_parts.append(num / den)
        b_end = br_h[:, end:end + 1]
        g_r = b_end - br_h + li_r
        m_new = jnp.maximum(b_end + m_h, jnp.max(g_r, axis=-1, keepdims=True))
        w_end = jnp.exp(g_r - m_new)
        decay = jnp.exp(b_end + m_h - m_new)
        w_end_rows.append(jnp.broadcast_to(w_end, (B_DH, ln)))
        decay_cols.append(jnp.broadcast_to(decay, (B_DH, 1)))
        w_end8.append(w_end)
        decay8.append(decay)
        m_new_row = jnp.where(lane128 == h, m_new, m_new_row)
    o_ref[b] = jnp.concatenate(h_parts, axis=0).T

    vtw = (vt * jnp.concatenate(w_end_rows, axis=0)).astype(BF16)
    ct_scr[...] = jnp.concatenate(decay_cols, axis=0) * ct_old + _dot(vtw, kb)
    pad4 = [jnp.zeros((1, ln), F32)] * (8 - N_HEADS)
    n_upd = _dot_f32_by_bf16(jnp.concatenate(w_end8 + pad4, axis=0), kb)
    own = _group_matrix(8, 1, hw, B_DH)
    n_scr[...] = (jnp.concatenate(decay8 + [jnp.zeros((1, 1), F32)] * (8 - N_HEADS), axis=0) * n_old
                  + own * n_upd)
    m_scr[0:1, :] = m_new_row


def _mlstm(pb, conv_w, conv_b, gate_b):
    hw = N_HEADS * B_DH
    return _scan_call(_mlstm_chain, pb, (conv_w, conv_b, gate_b), 2 * hw, hw,
                      ((hw, hw), (8, hw), (8, LANES)), "mlstm")


def _gla_chain(p_ref, wg_ref, bg_ref, o_ref, s_scr, *, b, rev, chunk, n_chunks, n_ctx_chunks):
    ln = p_ref.shape[1]
    kw_ = N_HEADS * C_DK
    vw_ = N_HEADS * C_DV
    c = GLA_SUB
    nb = ln // c
    z = 1 if rev else 0

    q = p_ref[b, :, 0:kw_] * (C_DK ** -0.5)
    k = p_ref[b, :, kw_:2 * kw_]
    v = p_ref[b, :, 2 * kw_:2 * kw_ + vw_]
    lr = p_ref[b, :, 2 * kw_ + vw_:2 * kw_ + vw_ + LANES]
    la = _log_sigmoid(_dot_bf16x3(lr, wg_ref[z]) + bg_ref[z]) * (1.0 / GLA_TAU)
    bc = _dot_bf16_by_f32(_tri(ln, rev), la)
    vb = v.astype(BF16)
    blk = _group_matrix(kw_, C_DK, vw_, C_DV)
    blk_b = blk.astype(BF16)

    s_old = s_scr[...]
    o_acc = _dot((q * jnp.exp(bc)).astype(BF16), s_old.astype(BF16))
    yield

    t_loc = lax.broadcasted_iota(jnp.int32, (c, kw_), 0)
    off_order = [jnp.where((t_loc >= t) if rev else (t_loc <= t), 0.0, NEG_BIG) for t in range(c)]
    pair_row = lax.broadcasted_iota(jnp.int32, (c, c * c), 0)
    pair_col = lax.broadcasted_iota(jnp.int32, (c, c * c), 1)
    sum_s = jnp.where((pair_col >= pair_row * c) & (pair_col < pair_row * c + c), 1.0, 0.0).astype(BF16)
    pair_sums = []
    for sub in range(nb):
        sl = slice(sub * c, (sub + 1) * c)
        bcb, qb_, kb_ = bc[sl], q[sl], k[sl]
        pieces = [(qb_[t:t + 1] * kb_) * jnp.exp(bcb[t:t + 1] - bcb + off_order[t]) for t in range(c)]
        e = jnp.concatenate(pieces, axis=0)
        pair_sums.append(_dot(e.astype(BF16), blk_b))
        if sub % 4 == 3:
            yield
    diag_parts = []
    for sub in range(nb):
        prod = (pair_sums[sub] * jnp.concatenate([v[sub * c:(sub + 1) * c]] * c, axis=0)).astype(BF16)
        diag_parts.append(_dot(sum_s, prod))
    yield
    o_acc = o_acc + jnp.concatenate(diag_parts, axis=0)

    rmask = lax.broadcasted_iota(jnp.int32, (N_HEADS * c, kw_), 0)
    lmask = _lane_iota((N_HEADS * c, kw_))
    rmask_v = lax.broadcasted_iota(jnp.int32, (N_HEADS * c, vw_), 0)
    lmask_v = _lane_iota((N_HEADS * c, vw_))
    same_k = jnp.zeros((N_HEADS * c, kw_), jnp.bool_)
    same_v = jnp.zeros((N_HEADS * c, vw_), jnp.bool_)
    for h in range(N_HEADS):
        same_k = same_k | ((rmask >= h * c) & (rmask < (h + 1) * c) & (lmask >= h * C_DK) & (lmask < (h + 1) * C_DK))
        same_v = same_v | ((rmask_v >= h * c) & (rmask_v < (h + 1) * c) & (lmask_v >= h * C_DV) & (lmask_v < (h + 1) * C_DV))
    off_full = jnp.zeros((ln, vw_), F32)
    for j in range(nb):
        if rev:
            if j == 0:
                continue
            rows = slice(0, j * c)
            edge = bc[j * c:j * c + 1]
        else:
            if j == nb - 1:
                continue
            rows = slice((j + 1) * c, ln)
            edge = bc[(j + 1) * c - 1:(j + 1) * c]
        sl = slice(j * c, (j + 1) * c)
        qs = q[rows] * jnp.exp(bc[rows] - edge)
        ks = k[sl] * jnp.exp(edge - bc[sl])
        kbd = jnp.where(same_k, jnp.concatenate([ks] * N_HEADS, axis=0), 0.0)
        vbd = jnp.where(same_v, jnp.concatenate([v[sl]] * N_HEADS, axis=0), 0.0)
        a = _dot_nt(qs.astype(BF16), kbd.astype(BF16))
        contrib = _dot(a.astype(BF16), vbd.astype(BF16))
        n_rows = contrib.shape[0]
        pad = jnp.zeros((ln - n_rows, vw_), F32)
        off_full = off_full + (jnp.concatenate([contrib, pad], axis=0) if rev
                               else jnp.concatenate([pad, contrib], axis=0))
        yield
    o_ref[b] = o_acc + off_full

    end = 0 if rev else ln - 1
    b_end = bc[end:end + 1]
    kd = k * jnp.exp(b_end - bc)
    upd = _dot(kd.T.astype(BF16), vb)
    eye = jnp.where(lax.broadcasted_iota(jnp.int32, (kw_, kw_), 0)
                    == lax.broadcasted_iota(jnp.int32, (kw_, kw_), 1), 1.0, 0.0).astype(F32)
    decay_col = jnp.sum(eye * jnp.exp(b_end), axis=-1, keepdims=True)
    s_scr[...] = decay_col * s_old + blk * upd


def _gla(pc, wg_pad, bg):
    return _scan_call(_gla_chain, pc, (wg_pad, bg), 0, N_HEADS * C_DV,
                      ((N_HEADS * C_DK, N_HEADS * C_DV),), "gla")


def _out_kernel(a_ref, bf_ref, bb_ref, bo_ref, cf_ref, cb_ref, d_ref, pg_ref, xc_ref, xl_ref, mod_ref,
                gn_ref, w_ref, o_ref, *, first_block):
    gmat = _group_matrix(GROUP_W, 64, GROUP_W, 64, 1.0 / 64).astype(BF16)
    gn = gn_ref[...]

    def rms64(hsum, gain):
        ms = _dot_f32_by_bf16(hsum * hsum, gmat)
        return hsum * lax.rsqrt(ms + EPS) * gain

    b_out = rms64(bf_ref[0] + bb_ref[0], gn[0:1]) * _sigmoid(bo_ref[0])
    c_out = rms64(cf_ref[0] + cb_ref[0], gn[1:2])
    y = jnp.concatenate([a_ref[0].astype(F32), b_out, c_out, d_ref[0].astype(F32)], axis=-1)
    gate = pg_ref[0].astype(F32)
    y = y * (gate * _sigmoid(gate))
    upd = _dot(y.astype(BF16), w_ref[...])
    x = _stream_tile(xc_ref, xl_ref, pl.program_id(1) + first_block)
    o_ref[0] = x + mod_ref[0, 0][2:3] * upd


def _out_projection(a_o, bf, bb, pb, cf, cb, d_o, pg, stream, msel, gains, w_out_b, with_ctx):
    bsz, r, d = pg.shape[0], pg.shape[1], msel.shape[-1]
    tm = ROW_TILE
    nct = CTX // tm
    first = 0 if with_ctx else nct
    row = lambda w: pl.BlockSpec((1, tm, w), lambda b, i: (b, i + first, 0))
    att = pl.BlockSpec((1, tm, 256), lambda b, i: (b, i, 0))
    out_rows = r - first * tm
    stream_specs, stream_args = _stream_specs(stream, tm, first)
    return pl.pallas_call(
        partial(_out_kernel, first_block=first),
        grid=(bsz, r // tm - first),
        in_specs=[att, row(256), row(256),
                  pl.BlockSpec((1, tm, 256), lambda b, i: (b, i + first, 3)),
                  row(256), row(256), att, row(PG_W), *stream_specs,
                  pl.BlockSpec((1, 1, 3, d), lambda b, i: (b, jnp.where(i + first >= nct, 1, 0), 0, 0)),
                  pl.BlockSpec((8, 256), lambda b, i: (0, 0)),
                  pl.BlockSpec((GROUP_W * 4, d), lambda b, i: (0, 0))],
        out_specs=pl.BlockSpec((1, tm, d), lambda b, i: (b, i, 0)),
        out_shape=jax.ShapeDtypeStruct((bsz, out_rows, d), F32),
        compiler_params=_cparams(("arbitrary", "arbitrary")),
        name="out_projection",
    )(a_o, bf, bb, pb, cf, cb, d_o, pg, *stream_args, msel, gains, w_out_b)


def _pad_cols(w, width):
    return jnp.pad(w, ((0, 0), (0, width - w.shape[1])))


def _layout_w_in(w):
    a = w[:, 0:768]
    bq, bk, bv = w[:, 768:1024], w[:, 1024:1280], w[:, 1280:1536]
    bg, bo = w[:, 1536:1552], w[:, 1552:1808]
    cq, ck, cv, clr = w[:, 1808:1936], w[:, 1936:2064], w[:, 2064:2320], w[:, 2320:2352]
    dd = w[:, 2352:2864]
    gg = w[:, 2864:3888]
    return jnp.concatenate([a, dd, bq, bk, bv, bo, _pad_cols(bg, LANES), cq, ck, cv, _pad_cols(clr, LANES), gg],
                           axis=1)


def _rope_tables(t, hd):
    nq = hd // 4
    per_tile = ROW_TILE // GRID_W
    inv = ROPE_THETA ** (-jnp.arange(nq, dtype=F32) / nq)
    lane = jnp.arange(256)
    within = (lane % hd) % (hd // 2)
    freq = inv[within % nq][None, :]
    sign = jnp.where(within < nq, -1.0, 1.0)[None, :]

    def tables(pos):
        ang = pos.astype(F32)[:, None] * freq
        return jnp.cos(ang), sign * jnp.sin(ang)

    row_cos, row_sin = tables(jnp.arange(t // GRID_W))
    col_cos, col_sin = tables(jnp.arange(GRID_W))

    def per_tile_blocks(tab, ctx_value):
        tab = jnp.pad(tab.reshape(-1, per_tile, 256), ((0, 0), (0, 8 - per_tile), (0, 0)))
        return jnp.concatenate([jnp.full((CTX // ROW_TILE, 8, 256), ctx_value, F32), tab], axis=0)

    return per_tile_blocks(row_cos, 1.0), per_tile_blocks(row_sin, 0.0), col_cos, col_sin


def _tile_lanes(g, width):
    return jnp.tile(g, width // g.shape[0])


def kernel(x, c, ctx, c_ctx, w_mod, b_mod, norm_g, w_in, w_out, a_qn, a_kn, a_lam, a_subln, b_conv_w, b_conv_b, b_gate_b, b_outn, c_wg, c_bg, c_outn, d_qn, d_kn, d_sink):
    bsz, t, d = x.shape
    depth = w_mod.shape[0]
    assert d == D_MODEL and ctx.shape[1] == CTX and t % ROW_TILE == 0 and bsz + 1 <= 8

    s_in = jnp.pad(jnp.concatenate([c, c_ctx[None, :]], axis=0), ((0, 8 - bsz - 1), (0, 0)))
    mod = _modulation(s_in, w_mod, b_mod).reshape(depth, 8, 3, d)
    rows8 = lambda *gains: jnp.pad(jnp.stack([_tile_lanes(g, 256) for g in gains]), ((0, 8 - len(gains)), (0, 0)))
    lane_row = lambda v: jnp.pad(v.reshape(1, -1), ((0, 0), (0, LANES - v.size)))
    tabs = _rope_tables(t, A_DK) + _rope_tables(t, D_HD)

    xc = (ctx, x)
    for l in range(depth):
        with_ctx = l < depth - 1
        lam_init = 0.8 - 0.6 * math.exp(-0.3 * l)
        msel = jnp.stack([jnp.broadcast_to(mod[l, bsz], (bsz, 3, d)), mod[l, :bsz]], axis=1)
        w_in_p = _layout_w_in(w_in[l]).astype(BF16)
        gains_att = rows8(a_qn[l], a_kn[l], d_qn[l], d_kn[l])
        pb, pc, pg, qta, ka, vta, qdt, kd, vdt = _in_projection(xc, msel, norm_g[l], w_in_p, tabs, gains_att)
        a_o = _diff_attention(qta, ka, vta, a_lam[l], a_subln[l], lam_init, with_ctx)
        d_o = _window_attention(qdt, kd, vdt, lane_row(d_sink[l]), with_ctx)
        bf, bb = _mlstm(pb, b_conv_w[l], b_conv_b[l].reshape(1, -1), lane_row(b_gate_b[l]))

        wg_pad = jnp.stack([jnp.pad(c_wg[l, z], ((z * GLA_RANK, LANES - (z + 1) * GLA_RANK), (0, 0)))
                            for z in range(2)])
        cf, cb = _gla(pc, wg_pad, c_bg[l].reshape(2, 1, LANES))

        xc = _out_projection(a_o, bf, bb, pb, cf, cb, d_o, pg, xc, msel, rows8(b_outn[l], c_outn[l]),
                             w_out[l].astype(BF16), with_ctx)
    return xc
```
